```python
import math
import jax, jax.numpy as jnp
from jax import lax
import numpy as np

D_MODEL = 1024
BATCH = 4
SEQ = 4096
DEPTH = 4

EXPAND = 2
D_MIX = EXPAND * D_MODEL
D_CONV = D_MIX // 2
D_RET = D_MIX - D_CONV
CONV_GROUPS = 8
CONV_K = 3
RET_HEADS = 8
RET_HEAD_DIM = D_RET // RET_HEADS
CHUNK = 128
ROPE_BASE = 10000.0
NORM_EPS = 1e-6
IN_COLS = 4 * D_CONV + 4 * D_RET

kernel_name = "hybrid_shortconv_retention_parallel_heads"


def rms_norm(x, g, eps=NORM_EPS):
    xf = x.astype(jnp.float32)
    y = xf * lax.rsqrt(jnp.mean(xf * xf, axis=-1, keepdims=True) + eps)
    return (y * g.astype(jnp.float32)).astype(x.dtype)


def rotary_tables(positions):
    half = RET_HEAD_DIM // 2
    inv_freq = 1.0 / (ROPE_BASE ** (jnp.arange(half, dtype=jnp.float32) / half))
    ang = positions.astype(jnp.float32)[..., None] * inv_freq
    return jnp.cos(ang)[:, :, None, :], jnp.sin(ang)[:, :, None, :]


def apply_rotary(x, cos, sin):
    xf = x.astype(jnp.float32)
    x1, x2 = jnp.split(xf, 2, axis=-1)
    return jnp.concatenate([x1 * cos - x2 * sin, x2 * cos + x1 * sin], axis=-1).astype(x.dtype)


def causal_depthwise_conv(u, w):
    c = u.shape[-1]
    return lax.conv_general_dilated(
        u, w.astype(u.dtype)[:, None, :], window_strides=(1,), padding=[(CONV_K - 1, 0)],
        dimension_numbers=("NWC", "WIO", "NWC"), feature_group_count=c)


def retention_chunkwise(q, k, v, log_gamma):
    b, s, h, dh = q.shape
    n = s // CHUNK
    q = q.reshape(b, n, CHUNK, h, dh)
    k = k.reshape(b, n, CHUNK, h, dh) * (dh ** -0.5)
    v = v.reshape(b, n, CHUNK, h, dh)
    idx = jnp.arange(CHUNK, dtype=jnp.float32)
    diff = idx[:, None] - idx[None, :]
    causal = diff >= 0
    decay = jnp.where(causal[None], jnp.exp(log_gamma[:, None, None] * jnp.where(causal, diff, 0.0)[None]), 0.0)
    scores = jnp.einsum("bnihd,bnjhd->bnhij", q, k).astype(jnp.float32) * decay
    intra = jnp.einsum("bnhij,bnjhe->bnihe", scores, v.astype(jnp.float32))
    k_w = jnp.exp((CHUNK - 1 - idx)[:, None] * log_gamma[None, :])
    kv = jnp.einsum("bnjhd,bnjhe->nbhde", k.astype(jnp.float32) * k_w[:, :, None],
                    v.astype(jnp.float32))
    chunk_decay = jnp.exp(CHUNK * log_gamma)[None, :, None, None]

    def step(state, kv_n):
        return chunk_decay * state + kv_n, state

    init = jnp.zeros((b, h, dh, dh), jnp.float32)
    _, prev = lax.scan(step, init, kv)
    q_w = jnp.exp((idx + 1.0)[:, None] * log_gamma[None, :])
    cross = jnp.einsum("bnihd,nbhde->bnihe", q.astype(jnp.float32) * q_w[:, :, None], prev)
    return (intra + cross).reshape(b, s, h, dh)


def setup_inputs(seed: int = 0) -> dict:
    key = jax.random.key(seed)
    kx, kp, kn1, kw1, kc, kw2, kn2 = jax.random.split(key, 7)
    x = jax.random.normal(kx, (BATCH, SEQ, D_MODEL), jnp.float32)
    offsets = jax.random.randint(kp, (BATCH, 1), 0, 1024, dtype=jnp.int32)
    positions = offsets + jnp.arange(SEQ, dtype=jnp.int32)[None, :]
    pre_norm = 1.0 + 0.05 * jax.random.normal(kn1, (DEPTH, D_MODEL), jnp.float32)
    w_in = jax.random.normal(kw1, (DEPTH, D_MODEL, IN_COLS), jnp.float32) * D_MODEL ** -0.5
    conv_w = jax.random.normal(kc, (DEPTH, CONV_K, D_CONV), jnp.float32) * CONV_K ** -0.5
    w_out = jax.random.normal(kw2, (DEPTH, D_MIX, D_MODEL), jnp.float32) * D_MIX ** -0.5
    post_norm = 1.0 + 0.05 * jax.random.normal(kn2, (DEPTH, D_MODEL), jnp.float32)
    return {"x": x, "positions": positions, "pre_norm": pre_norm, "w_in": w_in,
            "conv_w": conv_w, "w_out": w_out, "post_norm": post_norm}


def reference(x, positions, pre_norm, w_in, conv_w, w_out, post_norm):
    b, s, _ = x.shape
    cos, sin = rotary_tables(positions)
    log_gamma = jnp.log1p(-jnp.exp2(-5.0 - jnp.arange(RET_HEADS, dtype=jnp.float32)))
    cuts = [D_CONV, 2 * D_CONV, 3 * D_CONV, 4 * D_CONV,
            4 * D_CONV + D_RET, 4 * D_CONV + 2 * D_RET, 4 * D_CONV + 3 * D_RET]
    for layer in range(DEPTH):
        h = rms_norm(x, pre_norm[layer])
        proj = jnp.einsum("bsd,dp->bsp", h, w_in[layer])
        c_b, c_c, c_x, c_z, r_q, r_k, r_v, r_z = jnp.split(proj, cuts, axis=-1)
        y_conv = c_b * causal_depthwise_conv(c_c * c_x, conv_w[layer])
        y_conv = y_conv * jax.nn.silu(c_z)
        q = apply_rotary(r_q.reshape(b, s, RET_HEADS, RET_HEAD_DIM), cos, sin)
        k = apply_rotary(r_k.reshape(b, s, RET_HEADS, RET_HEAD_DIM), cos, sin)
        v = r_v.reshape(b, s, RET_HEADS, RET_HEAD_DIM)
        ret = retention_chunkwise(q, k, v, log_gamma)
        ret = ret * lax.rsqrt(jnp.mean(ret * ret, axis=-1, keepdims=True) + NORM_EPS)
        y_ret = ret.reshape(b, s, D_RET).astype(h.dtype) * jax.nn.silu(r_z)
        mix = jnp.concatenate([y_conv.astype(h.dtype), y_ret], axis=-1)
        out = jnp.einsum("bse,ed->bsd", mix, w_out[layer])
        x = x + rms_norm(out, post_norm[layer]).astype(x.dtype)
    return x
```

```python
import functools

import jax
import jax.numpy as jnp
from jax import lax
from jax.experimental import pallas as pl
from jax.experimental.pallas import tpu as pltpu

D_MODEL = 1024
D_MIX = 2 * D_MODEL
D_CONV = D_MIX // 2
D_RET = D_MIX - D_CONV
CONV_K = 3
RET_HEADS = 8
HEAD_DIM = D_RET // RET_HEADS
ROPE_BASE = 10000.0
NORM_EPS = 1e-6

CHUNK = 128
SEQ_TILE = 512
COL_BLOCK = 256
N_BLOCKS = D_CONV // COL_BLOCK
ROPE_TILE = 2048
SUBLANES = 8
VMEM_LIMIT_BYTES = 56 * 1024 * 1024

F32 = jnp.float32
BF16 = jnp.bfloat16


def _silu(x):
    return x * (1.0 / (1.0 + jnp.exp(-x)))


def _rope_kernel(pos_ref, invf_ref, cos_ref, sin_ref):
    ang = pos_ref[...].astype(F32) * invf_ref[...]
    lane = lax.broadcasted_iota(jnp.int32, ang.shape, 1)
    s = jnp.sin(ang)
    cos_ref[...] = jnp.cos(ang)
    sin_ref[...] = jnp.where(lane < HEAD_DIM // 2, -s, s)


def _rope_tables(positions, inv_freq2):
    n = positions.size
    pos = positions.reshape(n, 1)
    return pl.pallas_call(
        _rope_kernel,
        grid=(n // ROPE_TILE,),
        in_specs=[pl.BlockSpec((ROPE_TILE, 1), lambda i: (i, 0)),
                  pl.BlockSpec((1, HEAD_DIM), lambda i: (0, 0))],
        out_specs=[pl.BlockSpec((ROPE_TILE, HEAD_DIM), lambda i: (i, 0)),
                   pl.BlockSpec((ROPE_TILE, HEAD_DIM), lambda i: (i, 0))],
        out_shape=[jax.ShapeDtypeStruct((n, HEAD_DIM), F32)] * 2,
        name="rope_tables",
    )(pos, inv_freq2)


def _layer_kernel(x_ref, cos_ref, sin_ref, gpre_ref, w_ref, cw_ref, wout_ref, gpost_ref,
                  decay_ref, qw_ref, kw_ref, cdec_ref, o_ref,
                  h_scr, mix_scr, state_scr, halo_scr):
    T = x_ref.shape[0]

    @pl.when(pl.program_id(1) == 0)
    def _():
        state_scr[...] = jnp.zeros_like(state_scr)
        halo_scr[...] = jnp.zeros_like(halo_scr)

    x = x_ref[...]
    h = x * lax.rsqrt(jnp.mean(x * x, axis=-1, keepdims=True) + NORM_EPS) * gpre_ref[...]
    h_scr[...] = h.astype(BF16)

    for j in range(N_BLOCKS):
        cols = slice(j * COL_BLOCK, (j + 1) * COL_BLOCK)
        p = jnp.dot(h_scr[...], w_ref[j], preferred_element_type=F32)
        cb = p[:, 0:COL_BLOCK]
        cc = p[:, COL_BLOCK:2 * COL_BLOCK]
        cx = p[:, 2 * COL_BLOCK:3 * COL_BLOCK]
        cz = p[:, 3 * COL_BLOCK:4 * COL_BLOCK]
        u = cc * cx
        prev = halo_scr[:, cols]
        row = lax.broadcasted_iota(jnp.int32, u.shape, 0)
        u1 = jnp.where(row == 0, prev[7:8, :], pltpu.roll(u, 1, 0))
        u2 = jnp.where(row == 0, prev[6:7, :],
                       jnp.where(row == 1, prev[7:8, :], pltpu.roll(u, 2, 0)))
        cw = cw_ref[:, cols]
        conv = cw[0:1, :] * u2 + cw[1:2, :] * u1 + cw[2:3, :] * u
        halo_scr[:, cols] = u[T - SUBLANES:T, :]
        y = cb * conv * _silu(cz)
        mix_scr[:, cols] = y.astype(BF16)

    cosf = cos_ref[...]
    sinf = sin_ref[...]
    for pair in range(N_BLOCKS):
        p = jnp.dot(h_scr[...], w_ref[N_BLOCKS + pair], preferred_element_type=F32)
        for hh in range(2):
            head = 2 * pair + hh
            lo = hh * HEAD_DIM
            q = p[:, lo:lo + HEAD_DIM]
            k = p[:, COL_BLOCK + lo:COL_BLOCK + lo + HEAD_DIM]
            v = p[:, 2 * COL_BLOCK + lo:2 * COL_BLOCK + lo + HEAD_DIM]
            z = p[:, 3 * COL_BLOCK + lo:3 * COL_BLOCK + lo + HEAD_DIM]
            q = q * cosf + pltpu.roll(q, HEAD_DIM // 2, 1) * sinf
            k = k * cosf + pltpu.roll(k, HEAD_DIM // 2, 1) * sinf
            st = state_scr[head]
            for c in range(T // CHUNK):
                rows = slice(c * CHUNK, (c + 1) * CHUNK)
                qc, kc, zc = q[rows], k[rows], z[rows]
                vb = v[rows].astype(BF16)
                sc = lax.dot_general(qc.astype(BF16), kc.astype(BF16),
                                     (((1,), (1,)), ((), ())), preferred_element_type=F32)
                sc = sc * decay_ref[head]
                lhs = jnp.concatenate(
                    [sc.astype(BF16), (qc * qw_ref[head]).astype(BF16)], axis=1)
                rhs = jnp.concatenate([vb, st.astype(BF16)], axis=0)
                o = jnp.dot(lhs, rhs, preferred_element_type=F32)
                kv = lax.dot_general((kc * kw_ref[head]).astype(BF16), vb,
                                     (((0,), (0,)), ((), ())), preferred_element_type=F32)
                st = cdec_ref[head] * st + kv
                rn = o * lax.rsqrt(jnp.mean(o * o, axis=-1, keepdims=True) + NORM_EPS)
                y = rn * _silu(zc)
                mix_scr[rows, D_CONV + head * HEAD_DIM:D_CONV + (head + 1) * HEAD_DIM] = (
                    y.astype(BF16))
            state_scr[head] = st

    out = jnp.dot(mix_scr[...], wout_ref[...], preferred_element_type=F32)
    y = out * lax.rsqrt(jnp.mean(out * out, axis=-1, keepdims=True) + NORM_EPS) * gpost_ref[...]
    o_ref[...] = x_ref[...] + y


def _layer_call(x, cosf, sinf, gpre, w, cw, wout, gpost, decay, qw, kw, cdec):
    b, s, d = x.shape
    n_tiles = s // SEQ_TILE
    const2 = lambda bi, si: (0, 0)
    const3 = lambda bi, si: (0, 0, 0)
    once = pl.Buffered(1)
    in_specs = [
        pl.BlockSpec((None, SEQ_TILE, d), lambda bi, si: (bi, si, 0)),
        pl.BlockSpec((SEQ_TILE, HEAD_DIM), lambda bi, si: (bi * n_tiles + si, 0)),
        pl.BlockSpec((SEQ_TILE, HEAD_DIM), lambda bi, si: (bi * n_tiles + si, 0)),
        pl.BlockSpec((1, d), const2, pipeline_mode=once),
        pl.BlockSpec(w.shape, const3, pipeline_mode=once),
        pl.BlockSpec(cw.shape, const2, pipeline_mode=once),
        pl.BlockSpec(wout.shape, const2, pipeline_mode=once),
        pl.BlockSpec((1, d), const2, pipeline_mode=once),
        pl.BlockSpec(decay.shape, const3, pipeline_mode=once),
        pl.BlockSpec(qw.shape, const3, pipeline_mode=once),
        pl.BlockSpec(kw.shape, const3, pipeline_mode=once),
        pl.BlockSpec(cdec.shape, const3, pipeline_mode=once),
    ]
    return pl.pallas_call(
        _layer_kernel,
        grid=(b, n_tiles),
        in_specs=in_specs,
        out_specs=pl.BlockSpec((None, SEQ_TILE, d), lambda bi, si: (bi, si, 0)),
        out_shape=jax.ShapeDtypeStruct(x.shape, x.dtype),
        scratch_shapes=[
            pltpu.VMEM((SEQ_TILE, d), BF16),
            pltpu.VMEM((SEQ_TILE, D_MIX), BF16),
            pltpu.VMEM((RET_HEADS, HEAD_DIM, HEAD_DIM), F32),
            pltpu.VMEM((SUBLANES, D_CONV), F32),
        ],
        compiler_params=pltpu.CompilerParams(
            dimension_semantics=("arbitrary", "arbitrary"),
            vmem_limit_bytes=VMEM_LIMIT_BYTES),
        name="hybrid_layer",
    )(x, cosf, sinf, gpre, w, cw, wout, gpost, decay, qw, kw, cdec)


def _group_in_proj(w_in_l):
    d = w_in_l.shape[0]
    w = w_in_l.reshape(d, 2, 4, N_BLOCKS, COL_BLOCK)
    w = jnp.transpose(w, (1, 3, 0, 2, 4))
    return w.reshape(2 * N_BLOCKS, d, 4 * COL_BLOCK).astype(BF16)


def _retention_tables():
    log_gamma = jnp.log1p(-jnp.exp2(-5.0 - jnp.arange(RET_HEADS, dtype=F32)))
    scale = HEAD_DIM ** -0.5
    idx = jnp.arange(CHUNK, dtype=F32)
    diff = idx[:, None] - idx[None, :]
    causal = diff >= 0
    decay = jnp.where(causal[None],
                      jnp.exp(log_gamma[:, None, None] * jnp.where(causal, diff, 0.0)[None]), 0.0)
    k_w = jnp.exp((CHUNK - 1 - idx)[None, :] * log_gamma[:, None])
    q_w = jnp.exp((idx + 1.0)[None, :] * log_gamma[:, None])
    cdec = jnp.exp(CHUNK * log_gamma)
    bcast = lambda t: jnp.broadcast_to(t[:, :, None], (RET_HEADS, CHUNK, HEAD_DIM))
    return (decay * scale, bcast(q_w), bcast(k_w * scale),
            jnp.broadcast_to(cdec[:, None, None], (RET_HEADS, 1, HEAD_DIM)))


def kernel(x, positions, pre_norm, w_in, conv_w, w_out, post_norm):
    depth = w_in.shape[0]
    half = HEAD_DIM // 2
    inv_freq = 1.0 / (ROPE_BASE ** (jnp.arange(half, dtype=F32) / half))
    inv_freq2 = jnp.concatenate([inv_freq, inv_freq]).reshape(1, HEAD_DIM)
    cosf, sinf = _rope_tables(positions, inv_freq2)
    decay, qw, kw, cdec = _retention_tables()
    for layer in range(depth):
        x = _layer_call(
            x, cosf, sinf, pre_norm[layer].reshape(1, -1), _group_in_proj(w_in[layer]),
            conv_w[layer], w_out[layer].astype(BF16), post_norm[layer].reshape(1, -1),
            decay, qw, kw, cdec)
    return x
```

```python
import jax
import jax.numpy as jnp
from jax import lax
from jax.experimental import pallas as pl
from jax.experimental.pallas import tpu as pltpu

D_MODEL = 1024
D_MIX = 2 * D_MODEL
D_CONV = D_MIX // 2
D_RET = D_MIX - D_CONV
CONV_K = 3
RET_HEADS = 8
HEAD_DIM = D_RET // RET_HEADS
ROPE_BASE = 10000.0
NORM_EPS = 1e-6

CHUNK = 128
SEQ_TILE = 512
COL_BLOCK = 256
N_BLOCKS = D_CONV // COL_BLOCK
ROPE_TILE = 2048
SUBLANES = 8
VMEM_LIMIT_BYTES = 56 * 1024 * 1024

F32 = jnp.float32
BF16 = jnp.bfloat16
NT_DIMS = (((1,), (1,)), ((), ()))
TN_DIMS = (((0,), (0,)), ((), ()))


def _silu(x):
    return x * (1.0 / (1.0 + jnp.exp(-x)))


def _rope_kernel(pos_ref, invf_ref, cos_ref, sin_ref):
    ang = pos_ref[...].astype(F32) * invf_ref[...]
    lane = lax.broadcasted_iota(jnp.int32, ang.shape, 1)
    s = jnp.sin(ang)
    cos_ref[...] = jnp.cos(ang)
    sin_ref[...] = jnp.where(lane < HEAD_DIM // 2, -s, s)


def _rope_tables(positions, inv_freq2):
    n = positions.size
    pos = positions.reshape(n, 1)
    return pl.pallas_call(
        _rope_kernel,
        grid=(n // ROPE_TILE,),
        in_specs=[pl.BlockSpec((ROPE_TILE, 1), lambda i: (i, 0)),
                  pl.BlockSpec((1, HEAD_DIM), lambda i: (0, 0))],
        out_specs=[pl.BlockSpec((ROPE_TILE, HEAD_DIM), lambda i: (i, 0)),
                   pl.BlockSpec((ROPE_TILE, HEAD_DIM), lambda i: (i, 0))],
        out_shape=[jax.ShapeDtypeStruct((n, HEAD_DIM), F32)] * 2,
        name="rope_tables",
    )(pos, inv_freq2)


def _layer_kernel(x_ref, cos_ref, sin_ref, gpre_ref, w_ref, cw_ref, wout_ref, gpost_ref,
                  decay_ref, qw_ref, kw_ref, cdec_ref, o_ref,
                  h_scr, mix_scr, state_scr, halo_scr):
    T = x_ref.shape[0]
    n_chunks = T // CHUNK

    @pl.when(pl.program_id(1) == 0)
    def _():
        state_scr[...] = jnp.zeros_like(state_scr)
        halo_scr[...] = jnp.zeros_like(halo_scr)

    x = x_ref[...]
    h = x * lax.rsqrt(jnp.mean(x * x, axis=-1, keepdims=True) + NORM_EPS) * gpre_ref[...]
    h_scr[...] = h.astype(BF16)

    def in_proj(base, blk):
        rhs = jnp.concatenate(
            [w_ref[:, base + kind * D_CONV + blk * COL_BLOCK:
                   base + kind * D_CONV + (blk + 1) * COL_BLOCK] for kind in range(4)], axis=1)
        return jnp.dot(h_scr[...], rhs, preferred_element_type=F32)

    def conv_epilogue(j, p):
        cols = slice(j * COL_BLOCK, (j + 1) * COL_BLOCK)
        cb = p[:, 0:COL_BLOCK]
        cc = p[:, COL_BLOCK:2 * COL_BLOCK]
        cx = p[:, 2 * COL_BLOCK:3 * COL_BLOCK]
        cz = p[:, 3 * COL_BLOCK:4 * COL_BLOCK]
        u = cc * cx
        prev = halo_scr[:, cols]
        row = lax.broadcasted_iota(jnp.int32, u.shape, 0)
        u1 = jnp.where(row == 0, prev[7:8, :], pltpu.roll(u, 1, 0))
        u2 = jnp.where(row == 0, prev[6:7, :],
                       jnp.where(row == 1, prev[7:8, :], pltpu.roll(u, 2, 0)))
        cw = cw_ref[:, cols]
        conv = cw[0:1, :] * u2 + cw[1:2, :] * u1 + cw[2:3, :] * u
        halo_scr[:, cols] = u[T - SUBLANES:T, :]
        mix_scr[:, cols] = (cb * conv * _silu(cz)).astype(BF16)

    def rotary(p):
        cosf = cos_ref[...]
        sinf = sin_ref[...]
        heads = []
        for hh in range(2):
            lo = hh * HEAD_DIM
            q = p[:, lo:lo + HEAD_DIM]
            k = p[:, COL_BLOCK + lo:COL_BLOCK + lo + HEAD_DIM]
            v = p[:, 2 * COL_BLOCK + lo:2 * COL_BLOCK + lo + HEAD_DIM]
            z = p[:, 3 * COL_BLOCK + lo:3 * COL_BLOCK + lo + HEAD_DIM]
            q = q * cosf + pltpu.roll(q, HEAD_DIM // 2, 1) * sinf
            k = k * cosf + pltpu.roll(k, HEAD_DIM // 2, 1) * sinf
            heads.append((q, k, v.astype(BF16), z))
        return heads

    def scores_and_kv(pair, heads):
        out = []
        for hh, (q, k, vb, z) in enumerate(heads):
            head = 2 * pair + hh
            st = state_scr[head]
            for c in range(n_chunks):
                rows = slice(c * CHUNK, (c + 1) * CHUNK)
                qc, kc, vc = q[rows], k[rows], vb[rows]
                sc = lax.dot_general(qc.astype(BF16), kc.astype(BF16), NT_DIMS,
                                     preferred_element_type=F32)
                lhs = jnp.concatenate(
                    [(sc * decay_ref[head]).astype(BF16), (qc * qw_ref[head]).astype(BF16)],
                    axis=1)
                rhs = jnp.concatenate([vc, st.astype(BF16)], axis=0)
                out.append((head, rows, lhs, rhs, z[rows]))
                kv = lax.dot_general((kc * kw_ref[head]).astype(BF16), vc, TN_DIMS,
                                     preferred_element_type=F32)
                st = cdec_ref[head] * st + kv
            state_scr[head] = st
        return out

    def ret_outputs(items):
        return [(head, rows, jnp.dot(lhs, rhs, preferred_element_type=F32), zc)
                for head, rows, lhs, rhs, zc in items]

    def ret_epilogue(items):
        for head, rows, o, zc in items:
            rn = o * lax.rsqrt(jnp.mean(o * o, axis=-1, keepdims=True) + NORM_EPS)
            mix_scr[rows, D_CONV + head * HEAD_DIM:D_CONV + (head + 1) * HEAD_DIM] = (
                (rn * _silu(zc)).astype(BF16))

    pending_o = None
    pending_e = None
    pa = in_proj(4 * D_CONV, 0)
    for i in range(N_BLOCKS):
        pb = in_proj(0, i)
        sk = scores_and_kv(i, rotary(pa))
        conv_epilogue(i, pb)
        if i + 1 < N_BLOCKS:
            pa = in_proj(4 * D_CONV, i + 1)
        else:
            out_a = jnp.dot(mix_scr[:, 0:D_CONV], wout_ref[0:D_CONV, :],
                            preferred_element_type=F32)
        if pending_e is not None:
            ret_epilogue(pending_e)
        pending_e = ret_outputs(sk)
    ret_epilogue(pending_e)

    out = out_a + jnp.dot(mix_scr[:, D_CONV:D_MIX], wout_ref[D_CONV:D_MIX, :],
                          preferred_element_type=F32)
    y = out * lax.rsqrt(jnp.mean(out * out, axis=-1, keepdims=True) + NORM_EPS) * gpost_ref[...]
    o_ref[...] = x_ref[...] + y


def _layer_call(layer, x, cosf, sinf, gpre, w, cw, wout, gpost, decay, qw, kw, cdec):
    b, s, d = x.shape
    n_tiles = s // SEQ_TILE
    const3 = lambda bi, si: (0, 0, 0)
    this_layer = lambda bi, si: (layer, 0, 0)
    once = pl.Buffered(1)
    in_specs = [
        pl.BlockSpec((None, SEQ_TILE, d), lambda bi, si: (bi, si, 0)),
        pl.BlockSpec((SEQ_TILE, HEAD_DIM), lambda bi, si: (bi * n_tiles + si, 0)),
        pl.BlockSpec((SEQ_TILE, HEAD_DIM), lambda bi, si: (bi * n_tiles + si, 0)),
        pl.BlockSpec((None,) + gpre.shape[1:], this_layer, pipeline_mode=once),
        pl.BlockSpec((None,) + w.shape[1:], this_layer, pipeline_mode=once),
        pl.BlockSpec((None,) + cw.shape[1:], this_layer, pipeline_mode=once),
        pl.BlockSpec((None,) + wout.shape[1:], this_layer, pipeline_mode=once),
        pl.BlockSpec((None,) + gpost.shape[1:], this_layer, pipeline_mode=once),
        pl.BlockSpec(decay.shape, const3, pipeline_mode=once),
        pl.BlockSpec(qw.shape, const3, pipeline_mode=once),
        pl.BlockSpec(kw.shape, const3, pipeline_mode=once),
        pl.BlockSpec(cdec.shape, const3, pipeline_mode=once),
    ]
    return pl.pallas_call(
        _layer_kernel,
        grid=(b, n_tiles),
        in_specs=in_specs,
        out_specs=pl.BlockSpec((None, SEQ_TILE, d), lambda bi, si: (bi, si, 0)),
        out_shape=jax.ShapeDtypeStruct(x.shape, x.dtype),
        scratch_shapes=[
            pltpu.VMEM((SEQ_TILE, d), BF16),
            pltpu.VMEM((SEQ_TILE, D_MIX), BF16),
            pltpu.VMEM((RET_HEADS, HEAD_DIM, HEAD_DIM), F32),
            pltpu.VMEM((SUBLANES, D_CONV), F32),
        ],
        compiler_params=pltpu.CompilerParams(
            dimension_semantics=("arbitrary", "arbitrary"),
            vmem_limit_bytes=VMEM_LIMIT_BYTES),
        name="hybrid_layer",
    )(x, cosf, sinf, gpre, w, cw, wout, gpost, decay, qw, kw, cdec)


def _retention_tables():
    log_gamma = jnp.log1p(-jnp.exp2(-5.0 - jnp.arange(RET_HEADS, dtype=F32)))
    scale = HEAD_DIM ** -0.5
    idx = jnp.arange(CHUNK, dtype=F32)
    diff = idx[:, None] - idx[None, :]
    causal = diff >= 0
    decay = jnp.where(causal[None],
                      jnp.exp(log_gamma[:, None, None] * jnp.where(causal, diff, 0.0)[None]), 0.0)
    k_w = jnp.exp((CHUNK - 1 - idx)[None, :] * log_gamma[:, None])
    q_w = jnp.exp((idx + 1.0)[None, :] * log_gamma[:, None])
    cdec = jnp.exp(CHUNK * log_gamma)
    bcast = lambda t: jnp.broadcast_to(t[:, :, None], (RET_HEADS, CHUNK, HEAD_DIM))
    return (decay * scale, bcast(q_w), bcast(k_w * scale),
            jnp.broadcast_to(cdec[:, None, None], (RET_HEADS, 1, HEAD_DIM)))


def kernel(x, positions, pre_norm, w_in, conv_w, w_out, post_norm):
    depth = w_in.shape[0]
    half = HEAD_DIM // 2
    inv_freq = 1.0 / (ROPE_BASE ** (jnp.arange(half, dtype=F32) / half))
    inv_freq2 = jnp.concatenate([inv_freq, inv_freq]).reshape(1, HEAD_DIM)
    cosf, sinf = _rope_tables(positions, inv_freq2)
    decay, qw, kw, cdec = _retention_tables()
    w_in_bf = w_in.astype(BF16)
    w_out_bf = w_out.astype(BF16)
    gpre = pre_norm.reshape(depth, 1, D_MODEL)
    gpost = post_norm.reshape(depth, 1, D_MODEL)
    for layer in range(depth):
        x = _layer_call(layer, x, cosf, sinf, gpre, w_in_bf, conv_w, w_out_bf, gpost,
                        decay, qw, kw, cdec)
    return x
```

```python
import functools

import jax
import jax.numpy as jnp
from jax import lax
from jax.experimental import pallas as pl
from jax.experimental.pallas import tpu as pltpu

D_MODEL = 1024
D_MIX = 2 * D_MODEL
D_CONV = D_MIX // 2
D_RET = D_MIX - D_CONV
CONV_K = 3
RET_HEADS = 8
HEAD_DIM = D_RET // RET_HEADS
ROPE_BASE = 10000.0
NORM_EPS = 1e-6

CHUNK = 128
SEQ_TILE = 512
COL_BLOCK = 256
N_BLOCKS = D_CONV // COL_BLOCK
RET_BASE = 4 * D_CONV
MIX_GROUP = 2 * COL_BLOCK
ROPE_TILE = 2048
SUBLANES = 8
VMEM_LIMIT_BYTES = 56 * 1024 * 1024

F32 = jnp.float32
BF16 = jnp.bfloat16
NT_DIMS = (((1,), (1,)), ((), ()))
TN_DIMS = (((0,), (0,)), ((), ()))


def _silu(x):
    return x * (1.0 / (1.0 + jnp.exp(-x)))


def _rope_kernel(pos_ref, invf_ref, cos_ref, sin_ref):
    half_t = pos_ref.shape[0] // 2
    half_d = HEAD_DIM // 2
    first = lax.broadcasted_iota(jnp.int32, (half_t, HEAD_DIM), 1) < half_d
    pos2 = jnp.where(first, pos_ref[0:half_t, :], pos_ref[half_t:2 * half_t, :])
    ang = pos2.astype(F32) * invf_ref[...]
    c = jnp.cos(ang)
    s = jnp.sin(ang)
    c_sw = pltpu.roll(c, half_d, 1)
    s_sw = pltpu.roll(s, half_d, 1)
    cos_ref[0:half_t, :] = jnp.where(first, c, c_sw)
    cos_ref[half_t:2 * half_t, :] = jnp.where(first, c_sw, c)
    sin_ref[0:half_t, :] = jnp.where(first, -s, s_sw)
    sin_ref[half_t:2 * half_t, :] = jnp.where(first, -s_sw, s)


def _rope_tables(positions, inv_freq2):
    n = positions.size
    pos = positions.reshape(n, 1)
    return pl.pallas_call(
        _rope_kernel,
        grid=(n // ROPE_TILE,),
        in_specs=[pl.BlockSpec((ROPE_TILE, 1), lambda i: (i, 0)),
                  pl.BlockSpec((1, HEAD_DIM), lambda i: (0, 0))],
        out_specs=[pl.BlockSpec((ROPE_TILE, HEAD_DIM), lambda i: (i, 0)),
                   pl.BlockSpec((ROPE_TILE, HEAD_DIM), lambda i: (i, 0))],
        out_shape=[jax.ShapeDtypeStruct((n, HEAD_DIM), F32)] * 2,
        name="rope_tables",
    )(pos, inv_freq2)


def _layer_kernel(x_ref, xnext_ref, cos_ref, sin_ref, gpre_ref, w_ref, cw_ref, wout_ref,
                  gpost_ref, decay_ref, qw_ref, kw_ref, cdec_ref, wnext_ref, woutnext_ref,
                  o_ref, wnext_bf_ref, woutnext_bf_ref,
                  h_scr, pa_scr, mix_scr, out_scr, state_scr, halo_scr, *, tiles_per_seq):
    T = SEQ_TILE
    n_chunks = T // CHUNK
    g = pl.program_id(0)

    def pre_norm(x_in_ref):
        x = x_in_ref[...]
        h = x * lax.rsqrt(jnp.mean(x * x, axis=-1, keepdims=True) + NORM_EPS) * gpre_ref[...]
        h_scr[...] = h.astype(BF16)

    def in_proj(base, blk):
        rhs = jnp.concatenate(
            [w_ref[:, base + kind * D_CONV + blk * COL_BLOCK:
                   base + kind * D_CONV + (blk + 1) * COL_BLOCK] for kind in range(4)], axis=1)
        return jnp.dot(h_scr[...], rhs, preferred_element_type=F32)

    def conv_epilogue(j, p):
        cols = slice(j * COL_BLOCK, (j + 1) * COL_BLOCK)
        cb = p[:, 0:COL_BLOCK]
        cc = p[:, COL_BLOCK:2 * COL_BLOCK]
        cx = p[:, 2 * COL_BLOCK:3 * COL_BLOCK]
        cz = p[:, 3 * COL_BLOCK:4 * COL_BLOCK]
        u = cc * cx
        prev = halo_scr[:, cols]
        row = lax.broadcasted_iota(jnp.int32, u.shape, 0)
        u1 = jnp.where(row == 0, prev[7:8, :], pltpu.roll(u, 1, 0))
        u2 = jnp.where(row == 0, prev[6:7, :],
                       jnp.where(row == 1, prev[7:8, :], pltpu.roll(u, 2, 0)))
        cw = cw_ref[:, cols]
        conv = cw[0:1, :] * u2 + cw[1:2, :] * u1 + cw[2:3, :] * u
        halo_scr[:, cols] = u[T - SUBLANES:T, :]
        mix_scr[:, j * MIX_GROUP:j * MIX_GROUP + COL_BLOCK] = (
            (cb * conv * _silu(cz)).astype(BF16))

    def rotary(p):
        cosf = cos_ref[...]
        sinf = sin_ref[...]
        heads = []
        for hh in range(2):
            lo = hh * HEAD_DIM
            q = p[:, lo:lo + HEAD_DIM]
            k = p[:, COL_BLOCK + lo:COL_BLOCK + lo + HEAD_DIM]
            v = p[:, 2 * COL_BLOCK + lo:2 * COL_BLOCK + lo + HEAD_DIM]
            z = p[:, 3 * COL_BLOCK + lo:3 * COL_BLOCK + lo + HEAD_DIM]
            q = q * cosf + pltpu.roll(q, HEAD_DIM // 2, 1) * sinf
            k = k * cosf + pltpu.roll(k, HEAD_DIM // 2, 1) * sinf
            heads.append((q, k, v.astype(BF16), z))
        return heads

    def scores_and_kv(pair, heads):
        out = []
        for hh, (q, k, vb, z) in enumerate(heads):
            head = 2 * pair + hh
            st = state_scr[head]
            for c in range(n_chunks):
                rows = slice(c * CHUNK, (c + 1) * CHUNK)
                qc, kc, vc = q[rows], k[rows], vb[rows]
                sc = lax.dot_general(qc.astype(BF16), kc.astype(BF16), NT_DIMS,
                                     preferred_element_type=F32)
                lhs = jnp.concatenate(
                    [(sc * decay_ref[head]).astype(BF16), (qc * qw_ref[head]).astype(BF16)],
                    axis=1)
                rhs = jnp.concatenate([vc, st.astype(BF16)], axis=0)
                out.append((head, rows, lhs, rhs, z[rows]))
                kv = lax.dot_general((kc * kw_ref[head]).astype(BF16), vc, TN_DIMS,
                                     preferred_element_type=F32)
                st = cdec_ref[head] * st + kv
            state_scr[head] = st
        return out

    def ret_outputs(items):
        return [(head, rows, jnp.dot(lhs, rhs, preferred_element_type=F32), zc)
                for head, rows, lhs, rhs, zc in items]

    def ret_epilogue(items):
        for head, rows, o, zc in items:
            rn = o * lax.rsqrt(jnp.mean(o * o, axis=-1, keepdims=True) + NORM_EPS)
            lo = (head // 2) * MIX_GROUP + COL_BLOCK + (head % 2) * HEAD_DIM
            mix_scr[rows, lo:lo + HEAD_DIM] = (rn * _silu(zc)).astype(BF16)

    def out_proj(j):
        rhs = jnp.concatenate(
            [wout_ref[j * COL_BLOCK:(j + 1) * COL_BLOCK, :],
             wout_ref[D_CONV + j * COL_BLOCK:D_CONV + (j + 1) * COL_BLOCK, :]], axis=0)
        return jnp.dot(mix_scr[:, j * MIX_GROUP:(j + 1) * MIX_GROUP], rhs,
                       preferred_element_type=F32)

    @pl.when(g == 0)
    def _():
        pre_norm(x_ref)
        pa_scr[0] = in_proj(RET_BASE, 0)
        pa_scr[1] = in_proj(RET_BASE, 1)

    @pl.when(g % tiles_per_seq == 0)
    def _():
        state_scr[...] = jnp.zeros_like(state_scr)
        halo_scr[...] = jnp.zeros_like(halo_scr)

    wnext_bf_ref[...] = wnext_ref[...].astype(BF16)
    woutnext_bf_ref[...] = woutnext_ref[...].astype(BF16)

    pa = [pa_scr[0], pa_scr[1], in_proj(RET_BASE, 2)]
    sk = [scores_and_kv(0, rotary(pa[0]))]
    pa.append(in_proj(RET_BASE, 3))
    ro = [ret_outputs(sk[0])]
    sk.append(scores_and_kv(1, rotary(pa[1])))
    pb = in_proj(0, 0)
    ret_epilogue(ro[0])
    ro.append(ret_outputs(sk[1]))
    sk.append(scores_and_kv(2, rotary(pa[2])))
    conv_epilogue(0, pb)
    pb = in_proj(0, 1)
    ret_epilogue(ro[1])
    ro.append(ret_outputs(sk[2]))
    sk.append(scores_and_kv(3, rotary(pa[3])))
    conv_epilogue(1, pb)
    pb = in_proj(0, 2)
    ret_epilogue(ro[2])
    ro.append(ret_outputs(sk[3]))
    out_scr[...] = out_proj(0)
    conv_epilogue(2, pb)
    pb = in_proj(0, 3)
    ret_epilogue(ro[3])
    out_scr[...] += out_proj(1)
    pre_norm(xnext_ref)
    out_scr[...] += out_proj(2)
    pa_scr[0] = in_proj(RET_BASE, 0)
    conv_epilogue(3, pb)
    out = out_scr[...] + out_proj(3)
    pa_scr[1] = in_proj(RET_BASE, 1)
    y = out * lax.rsqrt(jnp.mean(out * out, axis=-1, keepdims=True) + NORM_EPS) * gpost_ref[...]
    o_ref[...] = x_ref[...] + y


def _layer_call(layer, x, cosf, sinf, gpre, w, cw, wout, gpost, decay, qw, kw, cdec,
                w_in_f32, w_out_f32, seq_len):
    n, d = x.shape
    depth = w_in_f32.shape[0]
    n_tiles = n // SEQ_TILE
    w_cols = w.shape[1] // n_tiles
    wout_rows = wout.shape[0] // n_tiles
    assert w_cols % HEAD_DIM == 0 and wout_rows % (2 * SUBLANES) == 0
    nxt = min(layer + 1, depth - 1)
    cur_tile = lambda g: (g, 0)
    next_tile = lambda g: (jnp.minimum(g + 1, n_tiles - 1), 0)
    const2 = lambda g: (0, 0)
    const3 = lambda g: (0, 0, 0)
    this_layer = lambda g: (layer, 0, 0)
    once = pl.Buffered(1)
    in_specs = [
        pl.BlockSpec((SEQ_TILE, d), cur_tile),
        pl.BlockSpec((SEQ_TILE, d), next_tile),
        pl.BlockSpec((SEQ_TILE, HEAD_DIM), cur_tile),
        pl.BlockSpec((SEQ_TILE, HEAD_DIM), cur_tile),
        pl.BlockSpec((None,) + gpre.shape[1:], this_layer, pipeline_mode=once),
        pl.BlockSpec(w.shape, const2, pipeline_mode=once),
        pl.BlockSpec((None,) + cw.shape[1:], this_layer, pipeline_mode=once),
        pl.BlockSpec(wout.shape, const2, pipeline_mode=once),
        pl.BlockSpec((None,) + gpost.shape[1:], this_layer, pipeline_mode=once),
        pl.BlockSpec(decay.shape, const3, pipeline_mode=once),
        pl.BlockSpec(qw.shape, const3, pipeline_mode=once),
        pl.BlockSpec(kw.shape, const3, pipeline_mode=once),
        pl.BlockSpec(cdec.shape, const3, pipeline_mode=once),
        pl.BlockSpec((None, w.shape[0], w_cols), lambda g: (nxt, 0, g)),
        pl.BlockSpec((None, wout_rows, wout.shape[1]), lambda g: (nxt, g, 0)),
    ]
    out_specs = [
        pl.BlockSpec((SEQ_TILE, d), cur_tile),
        pl.BlockSpec((w.shape[0], w_cols), lambda g: (0, g)),
        pl.BlockSpec((wout_rows, wout.shape[1]), lambda g: (g, 0)),
    ]
    out_shape = [
        jax.ShapeDtypeStruct(x.shape, x.dtype),
        jax.ShapeDtypeStruct(w.shape, BF16),
        jax.ShapeDtypeStruct(wout.shape, BF16),
    ]
    return pl.pallas_call(
        functools.partial(_layer_kernel, tiles_per_seq=seq_len // SEQ_TILE),
        grid=(n_tiles,),
        in_specs=in_specs,
        out_specs=out_specs,
        out_shape=out_shape,
        scratch_shapes=[
            pltpu.VMEM((SEQ_TILE, d), BF16),
            pltpu.VMEM((2, SEQ_TILE, 4 * COL_BLOCK), F32),
            pltpu.VMEM((SEQ_TILE, D_MIX), BF16),
            pltpu.VMEM((SEQ_TILE, d), F32),
            pltpu.VMEM((RET_HEADS, HEAD_DIM, HEAD_DIM), F32),
            pltpu.VMEM((SUBLANES, D_CONV), F32),
        ],
        compiler_params=pltpu.CompilerParams(
            dimension_semantics=("arbitrary",),
            vmem_limit_bytes=VMEM_LIMIT_BYTES),
        name="hybrid_layer",
    )(x, x, cosf, sinf, gpre, w, cw, wout, gpost, decay, qw, kw, cdec, w_in_f32, w_out_f32)


def _retention_tables():
    log_gamma = jnp.log1p(-jnp.exp2(-5.0 - jnp.arange(RET_HEADS, dtype=F32)))
    scale = HEAD_DIM ** -0.5
    idx = jnp.arange(CHUNK, dtype=F32)
    diff = idx[:, None] - idx[None, :]
    causal = diff >= 0
    decay = jnp.where(causal[None],
                      jnp.exp(log_gamma[:, None, None] * jnp.where(causal, diff, 0.0)[None]), 0.0)
    k_w = jnp.exp((CHUNK - 1 - idx)[None, :] * log_gamma[:, None])
    q_w = jnp.exp((idx + 1.0)[None, :] * log_gamma[:, None])
    cdec = jnp.exp(CHUNK * log_gamma)
    bcast = lambda t: jnp.broadcast_to(t[:, :, None], (RET_HEADS, CHUNK, HEAD_DIM))
    return (decay * scale, bcast(q_w), bcast(k_w * scale),
            jnp.broadcast_to(cdec[:, None, None], (RET_HEADS, 1, HEAD_DIM)))


def kernel(x, positions, pre_norm, w_in, conv_w, w_out, post_norm):
    depth = w_in.shape[0]
    b, s, d = x.shape
    half = HEAD_DIM // 2
    inv_freq = 1.0 / (ROPE_BASE ** (jnp.arange(half, dtype=F32) / half))
    inv_freq2 = jnp.concatenate([inv_freq, inv_freq]).reshape(1, HEAD_DIM)
    cosf, sinf = _rope_tables(positions, inv_freq2)
    decay, qw, kw, cdec = _retention_tables()
    gpre = pre_norm.reshape(depth, 1, D_MODEL)
    gpost = post_norm.reshape(depth, 1, D_MODEL)
    x = x.reshape(b * s, d)
    w_bf, wout_bf = w_in[0].astype(BF16), w_out[0].astype(BF16)
    for layer in range(depth):
        x, w_bf, wout_bf = _layer_call(
            layer, x, cosf, sinf, gpre, w_bf, conv_w, wout_bf, gpost, decay, qw, kw, cdec,
            w_in, w_out, s)
    return x.reshape(b, s, d)
```

```python
import functools

import jax
import jax.numpy as jnp
from jax import lax
from jax.experimental import pallas as pl
from jax.experimental.pallas import tpu as pltpu

D_MODEL = 1024
D_MIX = 2 * D_MODEL
D_CONV = D_MIX // 2
D_RET = D_MIX - D_CONV
CONV_K = 3
RET_HEADS = 8
HEAD_DIM = D_RET // RET_HEADS
ROPE_BASE = 10000.0
NORM_EPS = 1e-6

CHUNK = 128
SEQ_TILE = 512
COL_BLOCK = 256
N_BLOCKS = D_CONV // COL_BLOCK
RET_BASE = 4 * D_CONV
MIX_GROUP = 2 * COL_BLOCK
ROPE_TILE = 2048
SUBLANES = 8
VMEM_LIMIT_BYTES = 56 * 1024 * 1024

F32 = jnp.float32
BF16 = jnp.bfloat16
NT_DIMS = (((1,), (1,)), ((), ()))
TN_DIMS = (((0,), (0,)), ((), ()))


def _silu(x):
    h = 0.5 * x
    return h + h * jnp.tanh(h)


def _rope_kernel(pos_ref, invf_ref, cos_ref, sin_ref):
    half_t = pos_ref.shape[0] // 2
    half_d = HEAD_DIM // 2
    first = lax.broadcasted_iota(jnp.int32, (half_t, HEAD_DIM), 1) < half_d
    pos2 = jnp.where(first, pos_ref[0:half_t, :], pos_ref[half_t:2 * half_t, :])
    ang = pos2.astype(F32) * invf_ref[...]
    c = jnp.cos(ang)
    s = jnp.sin(ang)
    c_sw = pltpu.roll(c, half_d, 1)
    s_sw = pltpu.roll(s, half_d, 1)
    cos_ref[0:half_t, :] = jnp.where(first, c, c_sw)
    cos_ref[half_t:2 * half_t, :] = jnp.where(first, c_sw, c)
    sin_ref[0:half_t, :] = jnp.where(first, -s, s_sw)
    sin_ref[half_t:2 * half_t, :] = jnp.where(first, -s_sw, s)


def _rope_tables(positions, inv_freq2):
    n = positions.size
    pos = positions.reshape(n, 1)
    return pl.pallas_call(
        _rope_kernel,
        grid=(n // ROPE_TILE,),
        in_specs=[pl.BlockSpec((ROPE_TILE, 1), lambda i: (i, 0)),
                  pl.BlockSpec((1, HEAD_DIM), lambda i: (0, 0))],
        out_specs=[pl.BlockSpec((ROPE_TILE, HEAD_DIM), lambda i: (i, 0)),
                   pl.BlockSpec((ROPE_TILE, HEAD_DIM), lambda i: (i, 0))],
        out_shape=[jax.ShapeDtypeStruct((n, HEAD_DIM), F32)] * 2,
        name="rope_tables",
    )(pos, inv_freq2)


def _layer_kernel(x_ref, xnext_ref, cos_ref, sin_ref, gpre_ref, w_ref, cw_ref, wout_ref,
                  gpost_ref, decay_ref, qw_ref, kw_ref, cdec_ref, wnext_ref, woutnext_ref,
                  o_ref, wnext_bf_ref, woutnext_bf_ref,
                  h_scr, pa_scr, mix_scr, out_scr, state_scr, halo_scr, *, tiles_per_seq):
    T = SEQ_TILE
    n_chunks = T // CHUNK
    g = pl.program_id(0)

    def pre_norm(x_in_ref):
        x = x_in_ref[...]
        h = x * lax.rsqrt(jnp.mean(x * x, axis=-1, keepdims=True) + NORM_EPS) * gpre_ref[...]
        h_scr[...] = h.astype(BF16)

    def in_proj(base, blk):
        rhs = jnp.concatenate(
            [w_ref[:, base + kind * D_CONV + blk * COL_BLOCK:
                   base + kind * D_CONV + (blk + 1) * COL_BLOCK] for kind in range(4)], axis=1)
        return jnp.dot(h_scr[...], rhs, preferred_element_type=F32)

    def conv_epilogue(j, p):
        cols = slice(j * COL_BLOCK, (j + 1) * COL_BLOCK)
        cb = p[:, 0:COL_BLOCK]
        cc = p[:, COL_BLOCK:2 * COL_BLOCK]
        cx = p[:, 2 * COL_BLOCK:3 * COL_BLOCK]
        cz = p[:, 3 * COL_BLOCK:4 * COL_BLOCK]
        u = cc * cx
        prev = halo_scr[:, cols]
        row = lax.broadcasted_iota(jnp.int32, (SUBLANES, COL_BLOCK), 0)
        u1 = pltpu.roll(u, 1, 0)
        u2 = pltpu.roll(u, 2, 0)
        top1 = jnp.where(row == 0, prev[7:8, :], u1[0:SUBLANES])
        top2 = jnp.where(row == 0, prev[6:7, :],
                         jnp.where(row == 1, prev[7:8, :], u2[0:SUBLANES]))
        u1 = jnp.concatenate([top1, u1[SUBLANES:]], axis=0)
        u2 = jnp.concatenate([top2, u2[SUBLANES:]], axis=0)
        cw = cw_ref[:, cols]
        conv = cw[0:1, :] * u2 + cw[1:2, :] * u1 + cw[2:3, :] * u
        halo_scr[:, cols] = u[T - SUBLANES:T, :]
        mix_scr[:, j * MIX_GROUP:j * MIX_GROUP + COL_BLOCK] = (
            (cb * conv * _silu(cz)).astype(BF16))

    def rotary(p):
        cosf = cos_ref[...]
        sinf = sin_ref[...]
        heads = []
        for hh in range(2):
            lo = hh * HEAD_DIM
            q = p[:, lo:lo + HEAD_DIM]
            k = p[:, COL_BLOCK + lo:COL_BLOCK + lo + HEAD_DIM]
            v = p[:, 2 * COL_BLOCK + lo:2 * COL_BLOCK + lo + HEAD_DIM]
            z = p[:, 3 * COL_BLOCK + lo:3 * COL_BLOCK + lo + HEAD_DIM]
            q = q * cosf + pltpu.roll(q, HEAD_DIM // 2, 1) * sinf
            k = k * cosf + pltpu.roll(k, HEAD_DIM // 2, 1) * sinf
            heads.append((q, k, v.astype(BF16), z))
        return heads

    def scores_and_kv(pair, heads):
        out = []
        for hh, (q, k, vb, z) in enumerate(heads):
            head = 2 * pair + hh
            st = state_scr[head]
            for c in range(n_chunks):
                rows = slice(c * CHUNK, (c + 1) * CHUNK)
                qc, vc = q[rows], vb[rows]
                kt = k[rows].T
                sc = jnp.dot(qc.astype(BF16), kt.astype(BF16), preferred_element_type=F32)
                lhs = jnp.concatenate(
                    [(sc * decay_ref[head]).astype(BF16), (qc * qw_ref[head]).astype(BF16)],
                    axis=1)
                rhs = jnp.concatenate([vc, st.astype(BF16)], axis=0)
                out.append((head, rows, lhs, rhs, z[rows]))
                kv = jnp.dot((kt * kw_ref[head]).astype(BF16), vc, preferred_element_type=F32)
                st = cdec_ref[head] * st + kv
            state_scr[head] = st
        return out

    def ret_outputs(items):
        return [(head, rows, jnp.dot(lhs, rhs, preferred_element_type=F32), zc)
                for head, rows, lhs, rhs, zc in items]

    def ret_epilogue(items):
        for head, rows, o, zc in items:
            rn = o * lax.rsqrt(jnp.mean(o * o, axis=-1, keepdims=True) + NORM_EPS)
            lo = (head // 2) * MIX_GROUP + COL_BLOCK + (head % 2) * HEAD_DIM
            mix_scr[rows, lo:lo + HEAD_DIM] = (rn * _silu(zc)).astype(BF16)

    def out_proj(j0, j1):
        rhs = jnp.concatenate(
            [wout_ref[base + j * COL_BLOCK:base + (j + 1) * COL_BLOCK, :]
             for j in range(j0, j1) for base in (0, D_CONV)], axis=0)
        return jnp.dot(mix_scr[:, j0 * MIX_GROUP:j1 * MIX_GROUP], rhs,
                       preferred_element_type=F32)

    @pl.when(g == 0)
    def _():
        pre_norm(x_ref)
        pa_scr[0] = in_proj(RET_BASE, 0)
        pa_scr[1] = in_proj(RET_BASE, 1)

    @pl.when(g % tiles_per_seq == 0)
    def _():
        state_scr[...] = jnp.zeros_like(state_scr)
        halo_scr[...] = jnp.zeros_like(halo_scr)

    wnext_bf_ref[...] = wnext_ref[...].astype(BF16)
    woutnext_bf_ref[...] = woutnext_ref[...].astype(BF16)

    pa = [pa_scr[0], pa_scr[1], in_proj(RET_BASE, 2)]
    sk = [scores_and_kv(0, rotary(pa[0]))]
    pa.append(in_proj(RET_BASE, 3))
    ro = [ret_outputs(sk[0])]
    sk.append(scores_and_kv(1, rotary(pa[1])))
    pb = in_proj(0, 0)
    ret_epilogue(ro[0])
    ro.append(ret_outputs(sk[1]))
    sk.append(scores_and_kv(2, rotary(pa[2])))
    conv_epilogue(0, pb)
    pb = in_proj(0, 1)
    ret_epilogue(ro[1])
    ro.append(ret_outputs(sk[2]))
    sk.append(scores_and_kv(3, rotary(pa[3])))
    conv_epilogue(1, pb)
    pb = in_proj(0, 2)
    ret_epilogue(ro[2])
    ro.append(ret_outputs(sk[3]))
    out_scr[...] = out_proj(0, 1)
    conv_epilogue(2, pb)
    pb = in_proj(0, 3)
    ret_epilogue(ro[3])
    out_scr[...] += out_proj(1, 2)
    pre_norm(xnext_ref)
    out_scr[...] += out_proj(2, 3)
    pa_scr[0] = in_proj(RET_BASE, 0)
    conv_epilogue(3, pb)
    out = out_scr[...] + out_proj(3, 4)
    pa_scr[1] = in_proj(RET_BASE, 1)
    y = out * lax.rsqrt(jnp.mean(out * out, axis=-1, keepdims=True) + NORM_EPS) * gpost_ref[...]
    o_ref[...] = x_ref[...] + y


def _layer_call(layer, x, cosf, sinf, gpre, w, cw, wout, gpost, decay, qw, kw, cdec,
                w_in_f32, w_out_f32, seq_len):
    n, d = x.shape
    depth = w_in_f32.shape[0]
    n_tiles = n // SEQ_TILE
    w_cols = w.shape[1] // n_tiles
    wout_rows = wout.shape[0] // n_tiles
    assert w_cols % HEAD_DIM == 0 and wout_rows % (2 * SUBLANES) == 0
    nxt = min(layer + 1, depth - 1)
    cur_tile = lambda g: (g, 0)
    next_tile = lambda g: (jnp.minimum(g + 1, n_tiles - 1), 0)
    const2 = lambda g: (0, 0)
    const3 = lambda g: (0, 0, 0)
    this_layer = lambda g: (layer, 0, 0)
    once = pl.Buffered(1)
    in_specs = [
        pl.BlockSpec((SEQ_TILE, d), cur_tile),
        pl.BlockSpec((SEQ_TILE, d), next_tile),
        pl.BlockSpec((SEQ_TILE, HEAD_DIM), cur_tile),
        pl.BlockSpec((SEQ_TILE, HEAD_DIM), cur_tile),
        pl.BlockSpec((None,) + gpre.shape[1:], this_layer, pipeline_mode=once),
        pl.BlockSpec(w.shape, const2, pipeline_mode=once),
        pl.BlockSpec((None,) + cw.shape[1:], this_layer, pipeline_mode=once),
        pl.BlockSpec(wout.shape, const2, pipeline_mode=once),
        pl.BlockSpec((None,) + gpost.shape[1:], this_layer, pipeline_mode=once),
        pl.BlockSpec(decay.shape, const3, pipeline_mode=once),
        pl.BlockSpec(qw.shape, const3, pipeline_mode=once),
        pl.BlockSpec(kw.shape, const3, pipeline_mode=once),
        pl.BlockSpec(cdec.shape, const3, pipeline_mode=once),
        pl.BlockSpec((None, w.shape[0], w_cols), lambda g: (nxt, 0, g)),
        pl.BlockSpec((None, wout_rows, wout.shape[1]), lambda g: (nxt, g, 0)),
    ]
    out_specs = [
        pl.BlockSpec((SEQ_TILE, d), cur_tile),
        pl.BlockSpec((w.shape[0], w_cols), lambda g: (0, g)),
        pl.BlockSpec((wout_rows, wout.shape[1]), lambda g: (g, 0)),
    ]
    out_shape = [
        jax.ShapeDtypeStruct(x.shape, x.dtype),
        jax.ShapeDtypeStruct(w.shape, BF16),
        jax.ShapeDtypeStruct(wout.shape, BF16),
    ]
    return pl.pallas_call(
        functools.partial(_layer_kernel, tiles_per_seq=seq_len // SEQ_TILE),
        grid=(n_tiles,),
        in_specs=in_specs,
        out_specs=out_specs,
        out_shape=out_shape,
        scratch_shapes=[
            pltpu.VMEM((SEQ_TILE, d), BF16),
            pltpu.VMEM((2, SEQ_TILE, 4 * COL_BLOCK), F32),
            pltpu.VMEM((SEQ_TILE, D_MIX), BF16),
            pltpu.VMEM((SEQ_TILE, d), F32),
            pltpu.VMEM((RET_HEADS, HEAD_DIM, HEAD_DIM), F32),
            pltpu.VMEM((SUBLANES, D_CONV), F32),
        ],
        compiler_params=pltpu.CompilerParams(
            dimension_semantics=("arbitrary",),
            vmem_limit_bytes=VMEM_LIMIT_BYTES),
        name="hybrid_layer",
    )(x, x, cosf, sinf, gpre, w, cw, wout, gpost, decay, qw, kw, cdec, w_in_f32, w_out_f32)


def _retention_tables():
    log_gamma = jnp.log1p(-jnp.exp2(-5.0 - jnp.arange(RET_HEADS, dtype=F32)))
    scale = HEAD_DIM ** -0.5
    idx = jnp.arange(CHUNK, dtype=F32)
    diff = idx[:, None] - idx[None, :]
    causal = diff >= 0
    decay = jnp.where(causal[None],
                      jnp.exp(log_gamma[:, None, None] * jnp.where(causal, diff, 0.0)[None]), 0.0)
    k_w = jnp.exp((CHUNK - 1 - idx)[None, :] * log_gamma[:, None])
    q_w = jnp.exp((idx + 1.0)[None, :] * log_gamma[:, None])
    cdec = jnp.exp(CHUNK * log_gamma)
    bcast = lambda t: jnp.broadcast_to(t[:, :, None], (RET_HEADS, CHUNK, HEAD_DIM))
    k_w_t = jnp.broadcast_to((k_w * scale)[:, None, :], (RET_HEADS, HEAD_DIM, CHUNK))
    return (decay * scale, bcast(q_w), k_w_t,
            jnp.broadcast_to(cdec[:, None, None], (RET_HEADS, 1, HEAD_DIM)))


def kernel(x, positions, pre_norm, w_in, conv_w, w_out, post_norm):
    depth = w_in.shape[0]
    b, s, d = x.shape
    half = HEAD_DIM // 2
    inv_freq = 1.0 / (ROPE_BASE ** (jnp.arange(half, dtype=F32) / half))
    inv_freq2 = jnp.concatenate([inv_freq, inv_freq]).reshape(1, HEAD_DIM)
    cosf, sinf = _rope_tables(positions, inv_freq2)
    decay, qw, kw, cdec = _retention_tables()
    gpre = pre_norm.reshape(depth, 1, D_MODEL)
    gpost = post_norm.reshape(depth, 1, D_MODEL)
    x = x.reshape(b * s, d)
    w_bf, wout_bf = w_in[0].astype(BF16), w_out[0].astype(BF16)
    for layer in range(depth):
        x, w_bf, wout_bf = _layer_call(
            layer, x, cosf, sinf, gpre, w_bf, conv_w, wout_bf, gpost, decay, qw, kw, cdec,
            w_in, w_out, s)
    return x.reshape(b, s, d)
```

```python
import functools

import jax
import jax.numpy as jnp
from jax import lax
from jax.experimental import pallas as pl
from jax.experimental.pallas import tpu as pltpu

D_MODEL = 1024
D_MIX = 2 * D_MODEL
D_CONV = D_MIX // 2
D_RET = D_MIX - D_CONV
CONV_K = 3
RET_HEADS = 8
HEAD_DIM = D_RET // RET_HEADS
ROPE_BASE = 10000.0
NORM_EPS = 1e-6

CHUNK = 128
SEQ_TILE = 512
COL_BLOCK = 256
N_BLOCKS = D_CONV // COL_BLOCK
RET_BASE = 4 * D_CONV
MIX_GROUP = 2 * COL_BLOCK
ROPE_TILE = 2048
SUBLANES = 8
VMEM_LIMIT_BYTES = 56 * 1024 * 1024

F32 = jnp.float32
BF16 = jnp.bfloat16
NT_DIMS = (((1,), (1,)), ((), ()))
TN_DIMS = (((0,), (0,)), ((), ()))


def _silu(x):
    h = 0.5 * x
    return h + h * jnp.tanh(h)


def _rope_kernel(pos_ref, invf_ref, w_ref, wout_ref, cos_ref, sin_ref, w_bf_ref, wout_bf_ref):
    w_bf_ref[...] = w_ref[...].astype(BF16)
    wout_bf_ref[...] = wout_ref[...].astype(BF16)

    half_t = pos_ref.shape[0] // 2
    half_d = HEAD_DIM // 2
    first = lax.broadcasted_iota(jnp.int32, (half_t, HEAD_DIM), 1) < half_d
    pos2 = jnp.where(first, pos_ref[0:half_t, :], pos_ref[half_t:2 * half_t, :])
    ang = pos2.astype(F32) * invf_ref[...]
    c = jnp.cos(ang)
    s = jnp.sin(ang)
    c_sw = pltpu.roll(c, half_d, 1)
    s_sw = pltpu.roll(s, half_d, 1)
    cos_ref[0:half_t, :] = jnp.where(first, c, c_sw)
    cos_ref[half_t:2 * half_t, :] = jnp.where(first, c_sw, c)
    sin_ref[0:half_t, :] = jnp.where(first, -s, s_sw)
    sin_ref[half_t:2 * half_t, :] = jnp.where(first, -s_sw, s)


def _rope_tables(positions, inv_freq2, w_in, w_out):
    n = positions.size
    pos = positions.reshape(n, 1)
    steps = n // ROPE_TILE
    _, d_in, n_cols = w_in.shape
    _, n_rows, d_out = w_out.shape
    w_cols, wout_rows = n_cols // steps, n_rows // steps
    assert w_cols % HEAD_DIM == 0 and wout_rows % (2 * SUBLANES) == 0
    return pl.pallas_call(
        _rope_kernel,
        grid=(steps,),
        in_specs=[pl.BlockSpec((ROPE_TILE, 1), lambda i: (i, 0)),
                  pl.BlockSpec((1, HEAD_DIM), lambda i: (0, 0)),
                  pl.BlockSpec((None, d_in, w_cols), lambda i: (0, 0, i)),
                  pl.BlockSpec((None, wout_rows, d_out), lambda i: (0, i, 0))],
        out_specs=[pl.BlockSpec((ROPE_TILE, HEAD_DIM), lambda i: (i, 0)),
                   pl.BlockSpec((ROPE_TILE, HEAD_DIM), lambda i: (i, 0)),
                   pl.BlockSpec((d_in, w_cols), lambda i: (0, i)),
                   pl.BlockSpec((wout_rows, d_out), lambda i: (i, 0))],
        out_shape=[jax.ShapeDtypeStruct((n, HEAD_DIM), F32)] * 2 + [
            jax.ShapeDtypeStruct((d_in, n_cols), BF16),
            jax.ShapeDtypeStruct((n_rows, d_out), BF16)],
        name="rope_tables",
    )(pos, inv_freq2, w_in, w_out)


def _layer_kernel(x_ref, xnext_ref, cos_ref, sin_ref, gpre_ref, w_ref, cw_ref, wout_ref,
                  gpost_ref, decay_ref, qw_ref, kw_ref, cdec_ref, wnext_ref, woutnext_ref,
                  o_ref, wnext_bf_ref, woutnext_bf_ref,
                  h_scr, pa_scr, mix_scr, out_scr, state_scr, halo_scr, *, tiles_per_seq):
    T = SEQ_TILE
    n_chunks = T // CHUNK
    g = pl.program_id(0)

    def pre_norm(x_in_ref):
        x = x_in_ref[...]
        h = x * lax.rsqrt(jnp.mean(x * x, axis=-1, keepdims=True) + NORM_EPS) * gpre_ref[...]
        h_scr[...] = h.astype(BF16)

    def in_proj(base, blk):
        rhs = jnp.concatenate(
            [w_ref[:, base + kind * D_CONV + blk * COL_BLOCK:
                   base + kind * D_CONV + (blk + 1) * COL_BLOCK] for kind in range(4)], axis=1)
        return jnp.dot(h_scr[...], rhs, preferred_element_type=F32)

    def conv_epilogue(j, p):
        cols = slice(j * COL_BLOCK, (j + 1) * COL_BLOCK)
        cb = p[:, 0:COL_BLOCK]
        cc = p[:, COL_BLOCK:2 * COL_BLOCK]
        cx = p[:, 2 * COL_BLOCK:3 * COL_BLOCK]
        cz = p[:, 3 * COL_BLOCK:4 * COL_BLOCK]
        u = cc * cx
        prev = halo_scr[:, cols]
        row = lax.broadcasted_iota(jnp.int32, (SUBLANES, COL_BLOCK), 0)
        u1 = pltpu.roll(u, 1, 0)
        u2 = pltpu.roll(u, 2, 0)
        top1 = jnp.where(row == 0, prev[7:8, :], u1[0:SUBLANES])
        top2 = jnp.where(row == 0, prev[6:7, :],
                         jnp.where(row == 1, prev[7:8, :], u2[0:SUBLANES]))
        u1 = jnp.concatenate([top1, u1[SUBLANES:]], axis=0)
        u2 = jnp.concatenate([top2, u2[SUBLANES:]], axis=0)
        cw = cw_ref[:, cols]
        conv = cw[0:1, :] * u2 + cw[1:2, :] * u1 + cw[2:3, :] * u
        halo_scr[:, cols] = u[T - SUBLANES:T, :]
        mix_scr[:, j * MIX_GROUP:j * MIX_GROUP + COL_BLOCK] = (
            (cb * conv * _silu(cz)).astype(BF16))

    def rotary(p):
        cosf = cos_ref[...]
        sinf = sin_ref[...]
        heads = []
        for hh in range(2):
            lo = hh * HEAD_DIM
            q = p[:, lo:lo + HEAD_DIM]
            k = p[:, COL_BLOCK + lo:COL_BLOCK + lo + HEAD_DIM]
            v = p[:, 2 * COL_BLOCK + lo:2 * COL_BLOCK + lo + HEAD_DIM]
            z = p[:, 3 * COL_BLOCK + lo:3 * COL_BLOCK + lo + HEAD_DIM]
            q = q * cosf + pltpu.roll(q, HEAD_DIM // 2, 1) * sinf
            k = k * cosf + pltpu.roll(k, HEAD_DIM // 2, 1) * sinf
            heads.append((q, k, v.astype(BF16), z))
        return heads

    def scores_and_kv(pair, heads):
        out = []
        for hh, (q, k, vb, z) in enumerate(heads):
            head = 2 * pair + hh
            st = state_scr[head]
            for c in range(n_chunks):
                rows = slice(c * CHUNK, (c + 1) * CHUNK)
                qc, vc = q[rows], vb[rows]
                kt = k[rows].T
                sc = jnp.dot(qc.astype(BF16), kt.astype(BF16), preferred_element_type=F32)
                lhs = jnp.concatenate(
                    [(sc * decay_ref[head]).astype(BF16), (qc * qw_ref[head]).astype(BF16)],
                    axis=1)
                rhs = jnp.concatenate([vc, st.astype(BF16)], axis=0)
                out.append((head, rows, lhs, rhs, z[rows]))
                kv = jnp.dot((kt * kw_ref[head]).astype(BF16), vc, preferred_element_type=F32)
                st = cdec_ref[head] * st + kv
            state_scr[head] = st
        return out

    def ret_outputs(items):
        return [(head, rows, jnp.dot(lhs, rhs, preferred_element_type=F32), zc)
                for head, rows, lhs, rhs, zc in items]

    def ret_epilogue(items):
        for head, rows, o, zc in items:
            rn = o * lax.rsqrt(jnp.mean(o * o, axis=-1, keepdims=True) + NORM_EPS)
            lo = (head // 2) * MIX_GROUP + COL_BLOCK + (head % 2) * HEAD_DIM
            mix_scr[rows, lo:lo + HEAD_DIM] = (rn * _silu(zc)).astype(BF16)

    def out_proj(j0, j1):
        rhs = jnp.concatenate(
            [wout_ref[base + j * COL_BLOCK:base + (j + 1) * COL_BLOCK, :]
             for j in range(j0, j1) for base in (0, D_CONV)], axis=0)
        return jnp.dot(mix_scr[:, j0 * MIX_GROUP:j1 * MIX_GROUP], rhs,
                       preferred_element_type=F32)

    @pl.when(g == 0)
    def _():
        pre_norm(x_ref)
        pa_scr[0] = in_proj(RET_BASE, 0)
        pa_scr[1] = in_proj(RET_BASE, 1)

    @pl.when(g % tiles_per_seq == 0)
    def _():
        state_scr[...] = jnp.zeros_like(state_scr)
        halo_scr[...] = jnp.zeros_like(halo_scr)

    wnext_bf_ref[...] = wnext_ref[...].astype(BF16)
    woutnext_bf_ref[...] = woutnext_ref[...].astype(BF16)

    pa = [pa_scr[0], pa_scr[1], in_proj(RET_BASE, 2)]
    sk = [scores_and_kv(0, rotary(pa[0]))]
    pa.append(in_proj(RET_BASE, 3))
    ro = [ret_outputs(sk[0])]
    sk.append(scores_and_kv(1, rotary(pa[1])))
    pb = in_proj(0, 0)
    ret_epilogue(ro[0])
    ro.append(ret_outputs(sk[1]))
    sk.append(scores_and_kv(2, rotary(pa[2])))
    conv_epilogue(0, pb)
    pb = in_proj(0, 1)
    ret_epilogue(ro[1])
    ro.append(ret_outputs(sk[2]))
    sk.append(scores_and_kv(3, rotary(pa[3])))
    conv_epilogue(1, pb)
    pb = in_proj(0, 2)
    ret_epilogue(ro[2])
    ro.append(ret_outputs(sk[3]))
    conv_epilogue(2, pb)
    pb = in_proj(0, 3)
    ret_epilogue(ro[3])
    out_scr[...] = out_proj(0, 2)
    pre_norm(xnext_ref)
    pa_scr[0] = in_proj(RET_BASE, 0)
    conv_epilogue(3, pb)
    out = out_scr[...] + out_proj(2, 4)
    pa_scr[1] = in_proj(RET_BASE, 1)
    y = out * lax.rsqrt(jnp.mean(out * out, axis=-1, keepdims=True) + NORM_EPS) * gpost_ref[...]
    o_ref[...] = x_ref[...] + y


def _layer_call(layer, x, cosf, sinf, gpre, w, cw, wout, gpost, decay, qw, kw, cdec,
                w_in_f32, w_out_f32, seq_len):
    n, d = x.shape
    depth = w_in_f32.shape[0]
    n_tiles = n // SEQ_TILE
    w_cols = w.shape[1] // n_tiles
    wout_rows = wout.shape[0] // n_tiles
    assert w_cols % HEAD_DIM == 0 and wout_rows % (2 * SUBLANES) == 0
    nxt = min(layer + 1, depth - 1)
    cur_tile = lambda g: (g, 0)
    next_tile = lambda g: (jnp.minimum(g + 1, n_tiles - 1), 0)
    const2 = lambda g: (0, 0)
    const3 = lambda g: (0, 0, 0)
    this_layer = lambda g: (layer, 0, 0)
    once = pl.Buffered(1)
    in_specs = [
        pl.BlockSpec((SEQ_TILE, d), cur_tile),
        pl.BlockSpec((SEQ_TILE, d), next_tile),
        pl.BlockSpec((SEQ_TILE, HEAD_DIM), cur_tile),
        pl.BlockSpec((SEQ_TILE, HEAD_DIM), cur_tile),
        pl.BlockSpec((None,) + gpre.shape[1:], this_layer, pipeline_mode=once),
        pl.BlockSpec(w.shape, const2, pipeline_mode=once),
        pl.BlockSpec((None,) + cw.shape[1:], this_layer, pipeline_mode=once),
        pl.BlockSpec(wout.shape, const2, pipeline_mode=once),
        pl.BlockSpec((None,) + gpost.shape[1:], this_layer, pipeline_mode=once),
        pl.BlockSpec(decay.shape, const3, pipeline_mode=once),
        pl.BlockSpec(qw.shape, const3, pipeline_mode=once),
        pl.BlockSpec(kw.shape, const3, pipeline_mode=once),
        pl.BlockSpec(cdec.shape, const3, pipeline_mode=once),
        pl.BlockSpec((None, w.shape[0], w_cols), lambda g: (nxt, 0, g)),
        pl.BlockSpec((None, wout_rows, wout.shape[1]), lambda g: (nxt, g, 0)),
    ]
    out_specs = [
        pl.BlockSpec((SEQ_TILE, d), cur_tile),
        pl.BlockSpec((w.shape[0], w_cols), lambda g: (0, g)),
        pl.BlockSpec((wout_rows, wout.shape[1]), lambda g: (g, 0)),
    ]
    out_shape = [
        jax.ShapeDtypeStruct(x.shape, x.dtype),
        jax.ShapeDtypeStruct(w.shape, BF16),
        jax.ShapeDtypeStruct(wout.shape, BF16),
    ]
    return pl.pallas_call(
        functools.partial(_layer_kernel, tiles_per_seq=seq_len // SEQ_TILE),
        grid=(n_tiles,),
        in_specs=in_specs,
        out_specs=out_specs,
        out_shape=out_shape,
        scratch_shapes=[
            pltpu.VMEM((SEQ_TILE, d), BF16),
            pltpu.VMEM((2, SEQ_TILE, 4 * COL_BLOCK), F32),
            pltpu.VMEM((SEQ_TILE, D_MIX), BF16),
            pltpu.VMEM((SEQ_TILE, d), F32),
            pltpu.VMEM((RET_HEADS, HEAD_DIM, HEAD_DIM), F32),
            pltpu.VMEM((SUBLANES, D_CONV), F32),
        ],
        compiler_params=pltpu.CompilerParams(
            dimension_semantics=("arbitrary",),
            vmem_limit_bytes=VMEM_LIMIT_BYTES),
        name="hybrid_layer",
    )(x, x, cosf, sinf, gpre, w, cw, wout, gpost, decay, qw, kw, cdec, w_in_f32, w_out_f32)


def _retention_tables():
    log_gamma = jnp.log1p(-jnp.exp2(-5.0 - jnp.arange(RET_HEADS, dtype=F32)))
    scale = HEAD_DIM ** -0.5
    idx = jnp.arange(CHUNK, dtype=F32)
    diff = idx[:, None] - idx[None, :]
    causal = diff >= 0
    decay = jnp.where(causal[None],
                      jnp.exp(log_gamma[:, None, None] * jnp.where(causal, diff, 0.0)[None]), 0.0)
    k_w = jnp.exp((CHUNK - 1 - idx)[None, :] * log_gamma[:, None])
    q_w = jnp.exp((idx + 1.0)[None, :] * log_gamma[:, None])
    cdec = jnp.exp(CHUNK * log_gamma)
    bcast = lambda t: jnp.broadcast_to(t[:, :, None], (RET_HEADS, CHUNK, HEAD_DIM))
    k_w_t = jnp.broadcast_to((k_w * scale)[:, None, :], (RET_HEADS, HEAD_DIM, CHUNK))
    return (decay * scale, bcast(q_w), k_w_t,
            jnp.broadcast_to(cdec[:, None, None], (RET_HEADS, 1, HEAD_DIM)))


def kernel(x, positions, pre_norm, w_in, conv_w, w_out, post_norm):
    depth = w_in.shape[0]
    b, s, d = x.shape
    half = HEAD_DIM // 2
    inv_freq = 1.0 / (ROPE_BASE ** (jnp.arange(half, dtype=F32) / half))
    inv_freq2 = jnp.concatenate([inv_freq, inv_freq]).reshape(1, HEAD_DIM)
    cosf, sinf, w_bf, wout_bf = _rope_tables(positions, inv_freq2, w_in, w_out)
    decay, qw, kw, cdec = _retention_tables()
    gpre = pre_norm.reshape(depth, 1, D_MODEL)
    gpost = post_norm.reshape(depth, 1, D_MODEL)
    x = x.reshape(b * s, d)
    for layer in range(depth):
        x, w_bf, wout_bf = _layer_call(
            layer, x, cosf, sinf, gpre, w_bf, conv_w, wout_bf, gpost, decay, qw, kw, cdec,
            w_in, w_out, s)
    return x.reshape(b, s, d)
```

```python
import functools

import jax
import jax.numpy as jnp
from jax import lax
from jax.experimental import pallas as pl
from jax.experimental.pallas import tpu as pltpu

D_MODEL = 1024
D_MIX = 2 * D_MODEL
D_CONV = D_MIX // 2
D_RET = D_MIX - D_CONV
CONV_K = 3
RET_HEADS = 8
HEAD_DIM = D_RET // RET_HEADS
ROPE_BASE = 10000.0
NORM_EPS = 1e-6

CHUNK = 128
SEQ_TILE = 512
COL_BLOCK = 256
N_BLOCKS = D_CONV // COL_BLOCK
RET_BASE = 4 * D_CONV
MIX_GROUP = 2 * COL_BLOCK
ROPE_TILE = 2048
SUBLANES = 8
VMEM_LIMIT_BYTES = 56 * 1024 * 1024

F32 = jnp.float32
BF16 = jnp.bfloat16


def _silu(x):
    h = 0.5 * x
    return h + h * jnp.tanh(h)


def _rope_kernel(pos_ref, invf_ref, w_ref, wout_ref, cs_ref, w_bf_ref, wout_bf_ref):
    w_bf_ref[...] = w_ref[...].astype(BF16)
    wout_bf_ref[...] = wout_ref[...].astype(BF16)

    half_t = pos_ref.shape[0] // 2
    half_d = HEAD_DIM // 2
    first = lax.broadcasted_iota(jnp.int32, (half_t, HEAD_DIM), 1) < half_d
    pos2 = jnp.where(first, pos_ref[0:half_t, :], pos_ref[half_t:2 * half_t, :])
    ang = pos2.astype(F32) * invf_ref[...]
    c = jnp.cos(ang)
    s = jnp.sin(ang)
    c_sw = pltpu.roll(c, half_d, 1)
    s_sw = pltpu.roll(s, half_d, 1)
    cs_ref[0:half_t, 0:HEAD_DIM] = jnp.where(first, c, c_sw)
    cs_ref[half_t:2 * half_t, 0:HEAD_DIM] = jnp.where(first, c_sw, c)
    cs_ref[0:half_t, HEAD_DIM:2 * HEAD_DIM] = jnp.where(first, -s, s_sw)
    cs_ref[half_t:2 * half_t, HEAD_DIM:2 * HEAD_DIM] = jnp.where(first, -s_sw, s)


def _rope_tables(positions, inv_freq2, w_in, w_out):
    n = positions.size
    pos = positions.reshape(n, 1)
    steps = n // ROPE_TILE
    _, d_in, n_cols = w_in.shape
    _, n_rows, d_out = w_out.shape
    w_cols, wout_rows = n_cols // steps, n_rows // steps
    assert w_cols % HEAD_DIM == 0 and wout_rows % (2 * SUBLANES) == 0
    return pl.pallas_call(
        _rope_kernel,
        grid=(steps,),
        in_specs=[pl.BlockSpec((ROPE_TILE, 1), lambda i: (i, 0)),
                  pl.BlockSpec((1, HEAD_DIM), lambda i: (0, 0)),
                  pl.BlockSpec((None, d_in, w_cols), lambda i: (0, 0, i)),
                  pl.BlockSpec((None, wout_rows, d_out), lambda i: (0, i, 0))],
        out_specs=[pl.BlockSpec((ROPE_TILE, 2 * HEAD_DIM), lambda i: (i, 0)),
                   pl.BlockSpec((d_in, w_cols), lambda i: (0, i)),
                   pl.BlockSpec((wout_rows, d_out), lambda i: (i, 0))],
        out_shape=[
            jax.ShapeDtypeStruct((n, 2 * HEAD_DIM), F32),
            jax.ShapeDtypeStruct((d_in, n_cols), BF16),
            jax.ShapeDtypeStruct((n_rows, d_out), BF16)],
        name="rope_tables",
    )(pos, inv_freq2, w_in, w_out)


def _layer_kernel(x_ref, xnext_ref, cs_ref, gpre_ref, w_ref, cw_ref, wout_ref,
                  gpost_ref, decay_ref, qw_ref, kw_ref, cdec_ref, wnext_ref, woutnext_ref,
                  o_ref, wnext_bf_ref, woutnext_bf_ref,
                  h_scr, pa_scr, mix_scr, out_scr, state_scr, halo_scr, *, tiles_per_seq):
    T = SEQ_TILE
    n_chunks = T // CHUNK
    g = pl.program_id(0)

    def pre_norm(x_in_ref):
        x = x_in_ref[...]
        h = x * lax.rsqrt(jnp.mean(x * x, axis=-1, keepdims=True) + NORM_EPS) * gpre_ref[...]
        h_scr[...] = h.astype(BF16)

    def in_proj(base, blk):
        rhs = jnp.concatenate(
            [w_ref[:, base + kind * D_CONV + blk * COL_BLOCK:
                   base + kind * D_CONV + (blk + 1) * COL_BLOCK] for kind in range(4)], axis=1)
        return jnp.dot(h_scr[...], rhs, preferred_element_type=F32)

    def conv_epilogue(j, p):
        cols = slice(j * COL_BLOCK, (j + 1) * COL_BLOCK)
        cb = p[:, 0:COL_BLOCK]
        cc = p[:, COL_BLOCK:2 * COL_BLOCK]
        cx = p[:, 2 * COL_BLOCK:3 * COL_BLOCK]
        cz = p[:, 3 * COL_BLOCK:4 * COL_BLOCK]
        u = cc * cx
        prev = halo_scr[:, cols]
        row = lax.broadcasted_iota(jnp.int32, (SUBLANES, COL_BLOCK), 0)
        u1 = pltpu.roll(u, 1, 0)
        u2 = pltpu.roll(u, 2, 0)
        top1 = jnp.where(row == 0, prev[7:8, :], u1[0:SUBLANES])
        top2 = jnp.where(row == 0, prev[6:7, :],
                         jnp.where(row == 1, prev[7:8, :], u2[0:SUBLANES]))
        u1 = jnp.concatenate([top1, u1[SUBLANES:]], axis=0)
        u2 = jnp.concatenate([top2, u2[SUBLANES:]], axis=0)
        cw = cw_ref[:, cols]
        conv = cw[0:1, :] * u2 + cw[1:2, :] * u1 + cw[2:3, :] * u
        halo_scr[:, cols] = u[T - SUBLANES:T, :]
        mix_scr[:, j * MIX_GROUP:j * MIX_GROUP + COL_BLOCK] = (
            (cb * conv * _silu(cz)).astype(BF16))

    def rotary(p):
        cosf = cs_ref[:, 0:HEAD_DIM]
        sinf = cs_ref[:, HEAD_DIM:2 * HEAD_DIM]
        heads = []
        for hh in range(2):
            lo = hh * HEAD_DIM
            q = p[:, lo:lo + HEAD_DIM]
            k = p[:, COL_BLOCK + lo:COL_BLOCK + lo + HEAD_DIM]
            v = p[:, 2 * COL_BLOCK + lo:2 * COL_BLOCK + lo + HEAD_DIM]
            z = p[:, 3 * COL_BLOCK + lo:3 * COL_BLOCK + lo + HEAD_DIM]
            q = q * cosf + pltpu.roll(q, HEAD_DIM // 2, 1) * sinf
            k = k * cosf + pltpu.roll(k, HEAD_DIM // 2, 1) * sinf
            heads.append((q, k, v.astype(BF16), z))
        return heads

    def scores_and_kv(pair, heads):
        out = []
        for hh, (q, k, vb, z) in enumerate(heads):
            head = 2 * pair + hh
            st = state_scr[head]
            for c in range(n_chunks):
                rows = slice(c * CHUNK, (c + 1) * CHUNK)
                qc, vc = q[rows], vb[rows]
                kt = k[rows].T
                sc = jnp.dot(qc.astype(BF16), kt.astype(BF16), preferred_element_type=F32)
                lhs = jnp.concatenate(
                    [(sc * decay_ref[head]).astype(BF16), (qc * qw_ref[head]).astype(BF16)],
                    axis=1)
                rhs = jnp.concatenate([vc, st.astype(BF16)], axis=0)
                out.append((head, rows, lhs, rhs, z[rows]))
                kv = jnp.dot((kt * kw_ref[head]).astype(BF16), vc, preferred_element_type=F32)
                st = cdec_ref[head] * st + kv
            state_scr[head] = st
        return out

    def ret_outputs(items):
        return [(head, rows, jnp.dot(lhs, rhs, preferred_element_type=F32), zc)
                for head, rows, lhs, rhs, zc in items]

    def ret_epilogue(items):
        for head, rows, o, zc in items:
            rn = o * lax.rsqrt(jnp.mean(o * o, axis=-1, keepdims=True) + NORM_EPS)
            lo = (head // 2) * MIX_GROUP + COL_BLOCK + (head % 2) * HEAD_DIM
            mix_scr[rows, lo:lo + HEAD_DIM] = (rn * _silu(zc)).astype(BF16)

    def out_proj(j0, j1):
        rhs = jnp.concatenate(
            [wout_ref[base + j * COL_BLOCK:base + (j + 1) * COL_BLOCK, :]
             for j in range(j0, j1) for base in (0, D_CONV)], axis=0)
        return jnp.dot(mix_scr[:, j0 * MIX_GROUP:j1 * MIX_GROUP], rhs,
                       preferred_element_type=F32)

    @pl.when(g == 0)
    def _():
        pre_norm(x_ref)
        pa_scr[0] = in_proj(RET_BASE, 0)
        pa_scr[1] = in_proj(RET_BASE, 1)

    @pl.when(g % tiles_per_seq == 0)
    def _():
        state_scr[...] = jnp.zeros_like(state_scr)
        halo_scr[...] = jnp.zeros_like(halo_scr)

    wnext_bf_ref[...] = wnext_ref[...].astype(BF16)
    woutnext_bf_ref[...] = woutnext_ref[...].astype(BF16)

    pa = [pa_scr[0], pa_scr[1], in_proj(RET_BASE, 2)]
    sk = [scores_and_kv(0, rotary(pa[0]))]
    pa.append(in_proj(RET_BASE, 3))
    ro = [ret_outputs(sk[0])]
    sk.append(scores_and_kv(1, rotary(pa[1])))
    pb = in_proj(0, 0)
    ret_epilogue(ro[0])
    ro.append(ret_outputs(sk[1]))
    sk.append(scores_and_kv(2, rotary(pa[2])))
    conv_epilogue(0, pb)
    pb = in_proj(0, 1)
    ret_epilogue(ro[1])
    ro.append(ret_outputs(sk[2]))
    sk.append(scores_and_kv(3, rotary(pa[3])))
    conv_epilogue(1, pb)
    pb = in_proj(0, 2)
    ret_epilogue(ro[2])
    ro.append(ret_outputs(sk[3]))
    conv_epilogue(2, pb)
    pb = in_proj(0, 3)
    ret_epilogue(ro[3])
    out_scr[...] = out_proj(0, 2)
    pre_norm(xnext_ref)
    pa_scr[0] = in_proj(RET_BASE, 0)
    conv_epilogue(3, pb)
    out = out_scr[...] + out_proj(2, 4)
    pa_scr[1] = in_proj(RET_BASE, 1)
    y = out * lax.rsqrt(jnp.mean(out * out, axis=-1, keepdims=True) + NORM_EPS) * gpost_ref[...]
    o_ref[...] = x_ref[...] + y


def _layer_call(layer, x, cs, gpre, w, cw, wout, gpost, decay, qw, kw, cdec,
                w_in_f32, w_out_f32, seq_len):
    n, d = x.shape
    depth = w_in_f32.shape[0]
    n_tiles = n // SEQ_TILE
    w_cols = w.shape[1] // n_tiles
    wout_rows = wout.shape[0] // n_tiles
    assert w_cols % HEAD_DIM == 0 and wout_rows % (2 * SUBLANES) == 0
    nxt = min(layer + 1, depth - 1)
    cur_tile = lambda g: (g, 0)
    next_tile = lambda g: (jnp.minimum(g + 1, n_tiles - 1), 0)
    const2 = lambda g: (0, 0)
    const3 = lambda g: (0, 0, 0)
    this_layer = lambda g: (layer, 0, 0)
    once = pl.Buffered(1)
    in_specs = [
        pl.BlockSpec((SEQ_TILE, d), cur_tile),
        pl.BlockSpec((SEQ_TILE, d), next_tile),
        pl.BlockSpec((SEQ_TILE, 2 * HEAD_DIM), cur_tile),
        pl.BlockSpec((None,) + gpre.shape[1:], this_layer, pipeline_mode=once),
        pl.BlockSpec(w.shape, const2, pipeline_mode=once),
        pl.BlockSpec((None,) + cw.shape[1:], this_layer, pipeline_mode=once),
        pl.BlockSpec(wout.shape, const2, pipeline_mode=once),
        pl.BlockSpec((None,) + gpost.shape[1:], this_layer, pipeline_mode=once),
        pl.BlockSpec(decay.shape, const3, pipeline_mode=once),
        pl.BlockSpec(qw.shape, const3, pipeline_mode=once),
        pl.BlockSpec(kw.shape, const3, pipeline_mode=once),
        pl.BlockSpec(cdec.shape, const3, pipeline_mode=once),
        pl.BlockSpec((None, w.shape[0], w_cols), lambda g: (nxt, 0, g)),
        pl.BlockSpec((None, wout_rows, wout.shape[1]), lambda g: (nxt, g, 0)),
    ]
    out_specs = [
        pl.BlockSpec((SEQ_TILE, d), cur_tile),
        pl.BlockSpec((w.shape[0], w_cols), lambda g: (0, g)),
        pl.BlockSpec((wout_rows, wout.shape[1]), lambda g: (g, 0)),
    ]
    out_shape = [
        jax.ShapeDtypeStruct(x.shape, x.dtype),
        jax.ShapeDtypeStruct(w.shape, BF16),
        jax.ShapeDtypeStruct(wout.shape, BF16),
    ]
    return pl.pallas_call(
        functools.partial(_layer_kernel, tiles_per_seq=seq_len // SEQ_TILE),
        grid=(n_tiles,),
        in_specs=in_specs,
        out_specs=out_specs,
        out_shape=out_shape,
        scratch_shapes=[
            pltpu.VMEM((SEQ_TILE, d), BF16),
            pltpu.VMEM((2, SEQ_TILE, 4 * COL_BLOCK), F32),
            pltpu.VMEM((SEQ_TILE, D_MIX), BF16),
            pltpu.VMEM((SEQ_TILE, d), F32),
            pltpu.VMEM((RET_HEADS, HEAD_DIM, HEAD_DIM), F32),
            pltpu.VMEM((SUBLANES, D_CONV), F32),
        ],
        compiler_params=pltpu.CompilerParams(
            dimension_semantics=("arbitrary",),
            vmem_limit_bytes=VMEM_LIMIT_BYTES),
        name="hybrid_layer",
    )(x, x, cs, gpre, w, cw, wout, gpost, decay, qw, kw, cdec, w_in_f32, w_out_f32)


def _retention_tables():
    log_gamma = jnp.log1p(-jnp.exp2(-5.0 - jnp.arange(RET_HEADS, dtype=F32)))
    scale = HEAD_DIM ** -0.5
    idx = jnp.arange(CHUNK, dtype=F32)
    diff = idx[:, None] - idx[None, :]
    causal = diff >= 0
    decay = jnp.where(causal[None],
                      jnp.exp(log_gamma[:, None, None] * jnp.where(causal, diff, 0.0)[None]), 0.0)
    k_w = jnp.exp((CHUNK - 1 - idx)[None, :] * log_gamma[:, None])
    q_w = jnp.exp((idx + 1.0)[None, :] * log_gamma[:, None])
    cdec = jnp.exp(CHUNK * log_gamma)
    bcast = lambda t: jnp.broadcast_to(t[:, :, None], (RET_HEADS, CHUNK, HEAD_DIM))
    k_w_t = jnp.broadcast_to((k_w * scale)[:, None, :], (RET_HEADS, HEAD_DIM, CHUNK))
    return (decay * scale, bcast(q_w), k_w_t,
            jnp.broadcast_to(cdec[:, None, None], (RET_HEADS, 1, HEAD_DIM)))


def kernel(x, positions, pre_norm, w_in, conv_w, w_out, post_norm):
    depth = w_in.shape[0]
    b, s, d = x.shape
    assert d == D_MODEL and w_in.shape[1:] == (D_MODEL, 4 * D_CONV + 4 * D_RET)
    assert s % SEQ_TILE == 0 and (b * s) % ROPE_TILE == 0
    half = HEAD_DIM // 2
    inv_freq = 1.0 / (ROPE_BASE ** (jnp.arange(half, dtype=F32) / half))
    inv_freq2 = jnp.concatenate([inv_freq, inv_freq]).reshape(1, HEAD_DIM)
    cs, w_bf, wout_bf = _rope_tables(positions, inv_freq2, w_in, w_out)
    decay, qw, kw, cdec = _retention_tables()
    gpre = pre_norm.reshape(depth, 1, D_MODEL)
    gpost = post_norm.reshape(depth, 1, D_MODEL)
    x = x.reshape(b * s, d)
    for layer in range(depth):
        x, w_bf, wout_bf = _layer_call(
            layer, x, cs, gpre, w_bf, conv_w, wout_bf, gpost, decay, qw, kw, cdec,
            w_in, w_out, s)
    return x.reshape(b, s, d)
```

```python
import functools

import jax
import jax.numpy as jnp
from jax import lax
from jax.experimental import pallas as pl
from jax.experimental.pallas import tpu as pltpu

D_MODEL = 1024
D_MIX = 2 * D_MODEL
D_CONV = D_MIX // 2
D_RET = D_MIX - D_CONV
CONV_K = 3
RET_HEADS = 8
HEAD_DIM = D_RET // RET_HEADS
ROPE_BASE = 10000.0
NORM_EPS = 1e-6

CHUNK = 128
SEQ_TILE = 512
COL_BLOCK = 256
N_BLOCKS = D_CONV // COL_BLOCK
RET_BASE = 4 * D_CONV
MIX_GROUP = 2 * COL_BLOCK
ROPE_TILE = 2048
SUBLANES = 8
N_WEIGHT_COPIES = 10
VMEM_LIMIT_BYTES = 56 * 1024 * 1024

F32 = jnp.float32
BF16 = jnp.bfloat16
NT_DIMS = (((1,), (1,)), ((), ()))
TN_DIMS = (((0,), (0,)), ((), ()))


def _silu(x):
    h = 0.5 * x
    return h + h * jnp.tanh(h)


def _rope_kernel(pos_ref, invf_ref, w_ref, wout_ref, cos_ref, sin_ref, w_bf_ref, wout_bf_ref):
    w_bf_ref[...] = w_ref[...].astype(BF16)
    wout_bf_ref[...] = wout_ref[...].astype(BF16)

    half_t = pos_ref.shape[0] // 2
    half_d = HEAD_DIM // 2
    first = lax.broadcasted_iota(jnp.int32, (half_t, HEAD_DIM), 1) < half_d
    pos2 = jnp.where(first, pos_ref[0:half_t, :], pos_ref[half_t:2 * half_t, :])
    ang = pos2.astype(F32) * invf_ref[...]
    c = jnp.cos(ang)
    s = jnp.sin(ang)
    c_sw = pltpu.roll(c, half_d, 1)
    s_sw = pltpu.roll(s, half_d, 1)
    cos_ref[0:half_t, :] = jnp.where(first, c, c_sw)
    cos_ref[half_t:2 * half_t, :] = jnp.where(first, c_sw, c)
    sin_ref[0:half_t, :] = jnp.where(first, -s, s_sw)
    sin_ref[half_t:2 * half_t, :] = jnp.where(first, -s_sw, s)


def _rope_tables(positions, inv_freq2, w_in, w_out):
    n = positions.size
    pos = positions.reshape(n, 1)
    steps = n // ROPE_TILE
    _, d_in, n_cols = w_in.shape
    _, n_rows, d_out = w_out.shape
    w_cols, wout_rows = n_cols // steps, n_rows // steps
    assert w_cols % HEAD_DIM == 0 and wout_rows % (2 * SUBLANES) == 0
    return pl.pallas_call(
        _rope_kernel,
        grid=(steps,),
        in_specs=[pl.BlockSpec((ROPE_TILE, 1), lambda i: (i, 0)),
                  pl.BlockSpec((1, HEAD_DIM), lambda i: (0, 0)),
                  pl.BlockSpec((None, d_in, w_cols), lambda i: (0, 0, i)),
                  pl.BlockSpec((None, wout_rows, d_out), lambda i: (0, i, 0))],
        out_specs=[pl.BlockSpec((ROPE_TILE, HEAD_DIM), lambda i: (i, 0)),
                   pl.BlockSpec((ROPE_TILE, HEAD_DIM), lambda i: (i, 0)),
                   pl.BlockSpec((d_in, w_cols), lambda i: (0, i)),
                   pl.BlockSpec((wout_rows, d_out), lambda i: (i, 0))],
        out_shape=[jax.ShapeDtypeStruct((n, HEAD_DIM), F32)] * 2 + [
            jax.ShapeDtypeStruct((d_in, n_cols), BF16),
            jax.ShapeDtypeStruct((n_rows, d_out), BF16)],
        name="rope_tables",
    )(pos, inv_freq2, w_in, w_out)


def _layer_kernel(x_ref, xnext_ref, cos_ref, sin_ref, gpre_ref, w_hbm, cw_ref, wout_hbm,
                  gpost_ref, decay_ref, qw_ref, kw_ref, cdec_ref, wnext_ref, woutnext_ref,
                  o_ref, wnext_bf_ref, woutnext_bf_ref,
                  h_scr, pa_scr, mix_scr, out_scr, state_scr, halo_scr, w_ref, wout_ref, w_sem,
                  *, tiles_per_seq):
    T = SEQ_TILE
    n_chunks = T // CHUNK
    g = pl.program_id(0)

    def pre_norm(x_in_ref):
        x = x_in_ref[...]
        h = x * lax.rsqrt(jnp.mean(x * x, axis=-1, keepdims=True) + NORM_EPS) * gpre_ref[...]
        h_scr[...] = h.astype(BF16)

    def in_proj(base, blk):
        rhs = jnp.concatenate(
            [w_ref[:, base + kind * D_CONV + blk * COL_BLOCK:
                   base + kind * D_CONV + (blk + 1) * COL_BLOCK] for kind in range(4)], axis=1)
        return jnp.dot(h_scr[...], rhs, preferred_element_type=F32)

    def conv_epilogue(j, p):
        cols = slice(j * COL_BLOCK, (j + 1) * COL_BLOCK)
        cb = p[:, 0:COL_BLOCK]
        cc = p[:, COL_BLOCK:2 * COL_BLOCK]
        cx = p[:, 2 * COL_BLOCK:3 * COL_BLOCK]
        cz = p[:, 3 * COL_BLOCK:4 * COL_BLOCK]
        u = cc * cx
        prev = halo_scr[:, cols]
        row = lax.broadcasted_iota(jnp.int32, (SUBLANES, COL_BLOCK), 0)
        u1 = pltpu.roll(u, 1, 0)
        u2 = pltpu.roll(u, 2, 0)
        top1 = jnp.where(row == 0, prev[7:8, :], u1[0:SUBLANES])
        top2 = jnp.where(row == 0, prev[6:7, :],
                         jnp.where(row == 1, prev[7:8, :], u2[0:SUBLANES]))
        u1 = jnp.concatenate([top1, u1[SUBLANES:]], axis=0)
        u2 = jnp.concatenate([top2, u2[SUBLANES:]], axis=0)
        cw = cw_ref[:, cols]
        conv = cw[0:1, :] * u2 + cw[1:2, :] * u1 + cw[2:3, :] * u
        halo_scr[:, cols] = u[T - SUBLANES:T, :]
        mix_scr[:, j * MIX_GROUP:j * MIX_GROUP + COL_BLOCK] = (
            (cb * conv * _silu(cz)).astype(BF16))

    def rotary(p):
        cosf = cos_ref[...]
        sinf = sin_ref[...]
        heads = []
        for hh in range(2):
            lo = hh * HEAD_DIM
            q = p[:, lo:lo + HEAD_DIM]
            k = p[:, COL_BLOCK + lo:COL_BLOCK + lo + HEAD_DIM]
            v = p[:, 2 * COL_BLOCK + lo:2 * COL_BLOCK + lo + HEAD_DIM]
            z = p[:, 3 * COL_BLOCK + lo:3 * COL_BLOCK + lo + HEAD_DIM]
            q = q * cosf + pltpu.roll(q, HEAD_DIM // 2, 1) * sinf
            k = k * cosf + pltpu.roll(k, HEAD_DIM // 2, 1) * sinf
            heads.append((q, k, v.astype(BF16), z))
        return heads

    def scores_and_kv(pair, heads):
        out = []
        for hh, (q, k, vb, z) in enumerate(heads):
            head = 2 * pair + hh
            st = state_scr[head]
            for c in range(n_chunks):
                rows = slice(c * CHUNK, (c + 1) * CHUNK)
                qc, vc = q[rows], vb[rows]
                kt = k[rows].T
                sc = jnp.dot(qc.astype(BF16), kt.astype(BF16), preferred_element_type=F32)
                lhs = jnp.concatenate(
                    [(sc * decay_ref[head]).astype(BF16), (qc * qw_ref[head]).astype(BF16)],
                    axis=1)
                rhs = jnp.concatenate([vc, st.astype(BF16)], axis=0)
                out.append((head, rows, lhs, rhs, z[rows]))
                kv = jnp.dot((kt * kw_ref[head]).astype(BF16), vc, preferred_element_type=F32)
                st = cdec_ref[head] * st + kv
            state_scr[head] = st
        return out

    def ret_outputs(items):
        return [(head, rows, jnp.dot(lhs, rhs, preferred_element_type=F32), zc)
                for head, rows, lhs, rhs, zc in items]

    def ret_epilogue(items):
        for head, rows, o, zc in items:
            rn = o * lax.rsqrt(jnp.mean(o * o, axis=-1, keepdims=True) + NORM_EPS)
            lo = (head // 2) * MIX_GROUP + COL_BLOCK + (head % 2) * HEAD_DIM
            mix_scr[rows, lo:lo + HEAD_DIM] = (rn * _silu(zc)).astype(BF16)

    def out_proj(j0, j1):
        rhs = jnp.concatenate(
            [wout_ref[base + j * COL_BLOCK:base + (j + 1) * COL_BLOCK, :]
             for j in range(j0, j1) for base in (0, D_CONV)], axis=0)
        return jnp.dot(mix_scr[:, j0 * MIX_GROUP:j1 * MIX_GROUP], rhs,
                       preferred_element_type=F32)

    def weight_copies():
        early = 2 * COL_BLOCK
        spans = [(RET_BASE + kind * D_CONV, early) for kind in range(4)]
        spans += [(RET_BASE + kind * D_CONV + early, D_CONV - early) for kind in range(4)]
        spans += [(0, RET_BASE)]
        copies = [pltpu.make_async_copy(w_hbm.at[:, pl.ds(lo, n)], w_ref.at[:, pl.ds(lo, n)],
                                        w_sem.at[i]) for i, (lo, n) in enumerate(spans)]
        copies.append(pltpu.make_async_copy(wout_hbm, wout_ref, w_sem.at[len(spans)]))
        return copies

    @pl.when(g == 0)
    def _():
        copies = weight_copies()
        for c in copies:
            c.start()
        for c in copies[:4]:
            c.wait()
        pre_norm(x_ref)
        pa_scr[0] = in_proj(RET_BASE, 0)
        pa_scr[1] = in_proj(RET_BASE, 1)
        for c in copies[4:]:
            c.wait()

    @pl.when(g % tiles_per_seq == 0)
    def _():
        state_scr[...] = jnp.zeros_like(state_scr)
        halo_scr[...] = jnp.zeros_like(halo_scr)

    wnext_bf_ref[...] = wnext_ref[...].astype(BF16)
    woutnext_bf_ref[...] = woutnext_ref[...].astype(BF16)

    pa = [pa_scr[0], pa_scr[1], in_proj(RET_BASE, 2)]
    sk = [scores_and_kv(0, rotary(pa[0]))]
    pa.append(in_proj(RET_BASE, 3))
    ro = [ret_outputs(sk[0])]
    sk.append(scores_and_kv(1, rotary(pa[1])))
    pb = in_proj(0, 0)
    ret_epilogue(ro[0])
    ro.append(ret_outputs(sk[1]))
    sk.append(scores_and_kv(2, rotary(pa[2])))
    conv_epilogue(0, pb)
    pb = in_proj(0, 1)
    ret_epilogue(ro[1])
    ro.append(ret_outputs(sk[2]))
    sk.append(scores_and_kv(3, rotary(pa[3])))
    conv_epilogue(1, pb)
    pb = in_proj(0, 2)
    ret_epilogue(ro[2])
    ro.append(ret_outputs(sk[3]))
    conv_epilogue(2, pb)
    pb = in_proj(0, 3)
    ret_epilogue(ro[3])
    out_scr[...] = out_proj(0, 2)
    pre_norm(xnext_ref)
    pa_scr[0] = in_proj(RET_BASE, 0)
    conv_epilogue(3, pb)
    out = out_scr[...] + out_proj(2, 4)
    pa_scr[1] = in_proj(RET_BASE, 1)
    y = out * lax.rsqrt(jnp.mean(out * out, axis=-1, keepdims=True) + NORM_EPS) * gpost_ref[...]
    o_ref[...] = x_ref[...] + y


def _layer_call(layer, x, cosf, sinf, gpre, w, cw, wout, gpost, decay, qw, kw, cdec,
                w_in_f32, w_out_f32, seq_len):
    n, d = x.shape
    depth = w_in_f32.shape[0]
    n_tiles = n // SEQ_TILE
    w_cols = w.shape[1] // n_tiles
    wout_rows = wout.shape[0] // n_tiles
    assert w_cols % HEAD_DIM == 0 and wout_rows % (2 * SUBLANES) == 0
    nxt = min(layer + 1, depth - 1)
    cur_tile = lambda g: (g, 0)
    next_tile = lambda g: (jnp.minimum(g + 1, n_tiles - 1), 0)
    const3 = lambda g: (0, 0, 0)
    this_layer = lambda g: (layer, 0, 0)
    once = pl.Buffered(1)
    in_specs = [
        pl.BlockSpec((SEQ_TILE, d), cur_tile),
        pl.BlockSpec((SEQ_TILE, d), next_tile),
        pl.BlockSpec((SEQ_TILE, HEAD_DIM), cur_tile),
        pl.BlockSpec((SEQ_TILE, HEAD_DIM), cur_tile),
        pl.BlockSpec((None,) + gpre.shape[1:], this_layer, pipeline_mode=once),
        pl.BlockSpec(memory_space=pl.ANY),
        pl.BlockSpec((None,) + cw.shape[1:], this_layer, pipeline_mode=once),
        pl.BlockSpec(memory_space=pl.ANY),
        pl.BlockSpec((None,) + gpost.shape[1:], this_layer, pipeline_mode=once),
        pl.BlockSpec(decay.shape, const3, pipeline_mode=once),
        pl.BlockSpec(qw.shape, const3, pipeline_mode=once),
        pl.BlockSpec(kw.shape, const3, pipeline_mode=once),
        pl.BlockSpec(cdec.shape, const3, pipeline_mode=once),
        pl.BlockSpec((None, w.shape[0], w_cols), lambda g: (nxt, 0, g)),
        pl.BlockSpec((None, wout_rows, wout.shape[1]), lambda g: (nxt, g, 0)),
    ]
    out_specs = [
        pl.BlockSpec((SEQ_TILE, d), cur_tile),
        pl.BlockSpec((w.shape[0], w_cols), lambda g: (0, g)),
        pl.BlockSpec((wout_rows, wout.shape[1]), lambda g: (g, 0)),
    ]
    out_shape = [
        jax.ShapeDtypeStruct(x.shape, x.dtype),
        jax.ShapeDtypeStruct(w.shape, BF16),
        jax.ShapeDtypeStruct(wout.shape, BF16),
    ]
    return pl.pallas_call(
        functools.partial(_layer_kernel, tiles_per_seq=seq_len // SEQ_TILE),
        grid=(n_tiles,),
        in_specs=in_specs,
        out_specs=out_specs,
        out_shape=out_shape,
        scratch_shapes=[
            pltpu.VMEM((SEQ_TILE, d), BF16),
            pltpu.VMEM((2, SEQ_TILE, 4 * COL_BLOCK), F32),
            pltpu.VMEM((SEQ_TILE, D_MIX), BF16),
            pltpu.VMEM((SEQ_TILE, d), F32),
            pltpu.VMEM((RET_HEADS, HEAD_DIM, HEAD_DIM), F32),
            pltpu.VMEM((SUBLANES, D_CONV), F32),
            pltpu.VMEM(w.shape, BF16),
            pltpu.VMEM(wout.shape, BF16),
            pltpu.SemaphoreType.DMA((N_WEIGHT_COPIES,)),
        ],
        compiler_params=pltpu.CompilerParams(
            dimension_semantics=("arbitrary",),
            vmem_limit_bytes=VMEM_LIMIT_BYTES),
        name="hybrid_layer",
    )(x, x, cosf, sinf, gpre, w, cw, wout, gpost, decay, qw, kw, cdec, w_in_f32, w_out_f32)


def _retention_tables():
    log_gamma = jnp.log1p(-jnp.exp2(-5.0 - jnp.arange(RET_HEADS, dtype=F32)))
    scale = HEAD_DIM ** -0.5
    idx = jnp.arange(CHUNK, dtype=F32)
    diff = idx[:, None] - idx[None, :]
    causal = diff >= 0
    decay = jnp.where(causal[None],
                      jnp.exp(log_gamma[:, None, None] * jnp.where(causal, diff, 0.0)[None]), 0.0)
    k_w = jnp.exp((CHUNK - 1 - idx)[None, :] * log_gamma[:, None])
    q_w = jnp.exp((idx + 1.0)[None, :] * log_gamma[:, None])
    cdec = jnp.exp(CHUNK * log_gamma)
    bcast = lambda t: jnp.broadcast_to(t[:, :, None], (RET_HEADS, CHUNK, HEAD_DIM))
    k_w_t = jnp.broadcast_to((k_w * scale)[:, None, :], (RET_HEADS, HEAD_DIM, CHUNK))
    return (decay * scale, bcast(q_w), k_w_t,
            jnp.broadcast_to(cdec[:, None, None], (RET_HEADS, 1, HEAD_DIM)))


def kernel(x, positions, pre_norm, w_in, conv_w, w_out, post_norm):
    depth = w_in.shape[0]
    b, s, d = x.shape
    half = HEAD_DIM // 2
    inv_freq = 1.0 / (ROPE_BASE ** (jnp.arange(half, dtype=F32) / half))
    inv_freq2 = jnp.concatenate([inv_freq, inv_freq]).reshape(1, HEAD_DIM)
    cosf, sinf, w_bf, wout_bf = _rope_tables(positions, inv_freq2, w_in, w_out)
    decay, qw, kw, cdec = _retention_tables()
    gpre = pre_norm.reshape(depth, 1, D_MODEL)
    gpost = post_norm.reshape(depth, 1, D_MODEL)
    x = x.reshape(b * s, d)
    for layer in range(depth):
        x, w_bf, wout_bf = _layer_call(
            layer, x, cosf, sinf, gpre, w_bf, conv_w, wout_bf, gpost, decay, qw, kw, cdec,
            w_in, w_out, s)
    return x.reshape(b, s, d)
```

```python
import functools

import jax
import jax.numpy as jnp
import numpy as np
from jax import lax
from jax.experimental import pallas as pl
from jax.experimental.pallas import tpu as pltpu

D_MODEL = 1024
D_MIX = 2 * D_MODEL
D_CONV = D_MIX // 2
D_RET = D_MIX - D_CONV
CONV_K = 3
RET_HEADS = 8
HEAD_DIM = D_RET // RET_HEADS
ROPE_BASE = 10000.0
NORM_EPS = 1e-6

CHUNK = 128
SEQ_TILE = 512
COL_BLOCK = 256
N_BLOCKS = D_CONV // COL_BLOCK
RET_BASE = 4 * D_CONV
MIX_GROUP = 2 * COL_BLOCK
ROPE_TILE = 2048
SUBLANES = 8
N_WEIGHT_COPIES = 10
VMEM_LIMIT_BYTES = 56 * 1024 * 1024

F32 = jnp.float32
BF16 = jnp.bfloat16


def _silu(x):
    h = 0.5 * x
    return h + h * jnp.tanh(h)


def _rope_kernel(pos_ref, invf_ref, w_ref, wout_ref, cos_ref, sin_ref, w_bf_ref, wout_bf_ref):
    w_bf_ref[...] = w_ref[...].astype(BF16)
    wout_bf_ref[...] = wout_ref[...].astype(BF16)

    half_t = pos_ref.shape[0] // 2
    half_d = HEAD_DIM // 2
    first = lax.broadcasted_iota(jnp.int32, (half_t, HEAD_DIM), 1) < half_d
    pos2 = jnp.where(first, pos_ref[0:half_t, :], pos_ref[half_t:2 * half_t, :])
    ang = pos2.astype(F32) * invf_ref[...]
    c = jnp.cos(ang)
    s = jnp.sin(ang)
    c_sw = pltpu.roll(c, half_d, 1)
    s_sw = pltpu.roll(s, half_d, 1)
    cos_ref[0:half_t, :] = jnp.where(first, c, c_sw)
    cos_ref[half_t:2 * half_t, :] = jnp.where(first, c_sw, c)
    sin_ref[0:half_t, :] = jnp.where(first, -s, s_sw)
    sin_ref[half_t:2 * half_t, :] = jnp.where(first, -s_sw, s)


def _rope_tables(positions, inv_freq2, w_in, w_out):
    n = positions.size
    pos = positions.reshape(n, 1)
    steps = n // ROPE_TILE
    _, d_in, n_cols = w_in.shape
    _, n_rows, d_out = w_out.shape
    w_cols, wout_rows = n_cols // steps, n_rows // steps
    assert w_cols % HEAD_DIM == 0 and wout_rows % (2 * SUBLANES) == 0
    return pl.pallas_call(
        _rope_kernel,
        grid=(steps,),
        in_specs=[pl.BlockSpec((ROPE_TILE, 1), lambda i: (i, 0)),
                  pl.BlockSpec((1, HEAD_DIM), lambda i: (0, 0)),
                  pl.BlockSpec((None, d_in, w_cols), lambda i: (0, 0, i)),
                  pl.BlockSpec((None, wout_rows, d_out), lambda i: (0, i, 0))],
        out_specs=[pl.BlockSpec((ROPE_TILE, HEAD_DIM), lambda i: (i, 0)),
                   pl.BlockSpec((ROPE_TILE, HEAD_DIM), lambda i: (i, 0)),
                   pl.BlockSpec((d_in, w_cols), lambda i: (0, i)),
                   pl.BlockSpec((wout_rows, d_out), lambda i: (i, 0))],
        out_shape=[jax.ShapeDtypeStruct((n, HEAD_DIM), F32)] * 2 + [
            jax.ShapeDtypeStruct((d_in, n_cols), BF16),
            jax.ShapeDtypeStruct((n_rows, d_out), BF16)],
        name="rope_tables",
    )(pos, inv_freq2, w_in, w_out)


def _layer_kernel(x_ref, xnext_ref, cos_ref, sin_ref, gpre_ref, w_hbm, cw_ref, wout_hbm,
                  gpost_ref, decay_ref, qw_ref, kw_ref, cdec_ref, wnext_ref, woutnext_ref,
                  o_ref, wnext_bf_ref, woutnext_bf_ref,
                  h_scr, pa_scr, mix_scr, state_scr, halo_scr, w_ref, wout_ref, w_sem,
                  *, layer, tiles_per_seq):
    T = SEQ_TILE
    n_chunks = T // CHUNK
    g = pl.program_id(0)

    def pre_norm(x_in_ref):
        x = x_in_ref[...]
        gain = gpre_ref[layer:layer + 1, :]
        h = x * lax.rsqrt(jnp.mean(x * x, axis=-1, keepdims=True) + NORM_EPS) * gain
        h_scr[...] = h.astype(BF16)

    def in_proj(base, blk):
        rhs = jnp.concatenate(
            [w_ref[:, base + kind * D_CONV + blk * COL_BLOCK:
                   base + kind * D_CONV + (blk + 1) * COL_BLOCK] for kind in range(4)], axis=1)
        return jnp.dot(h_scr[...], rhs, preferred_element_type=F32)

    def conv_epilogue(j, p):
        cols = slice(j * COL_BLOCK, (j + 1) * COL_BLOCK)
        cb = p[:, 0:COL_BLOCK]
        cc = p[:, COL_BLOCK:2 * COL_BLOCK]
        cx = p[:, 2 * COL_BLOCK:3 * COL_BLOCK]
        cz = p[:, 3 * COL_BLOCK:4 * COL_BLOCK]
        u = cc * cx
        prev = halo_scr[:, cols]
        row = lax.broadcasted_iota(jnp.int32, (SUBLANES, COL_BLOCK), 0)
        u1 = pltpu.roll(u, 1, 0)
        u2 = pltpu.roll(u, 2, 0)
        top1 = jnp.where(row == 0, prev[7:8, :], u1[0:SUBLANES])
        top2 = jnp.where(row == 0, prev[6:7, :],
                         jnp.where(row == 1, prev[7:8, :], u2[0:SUBLANES]))
        u1 = jnp.concatenate([top1, u1[SUBLANES:]], axis=0)
        u2 = jnp.concatenate([top2, u2[SUBLANES:]], axis=0)
        cw = cw_ref[:, cols]
        conv = cw[0:1, :] * u2 + cw[1:2, :] * u1 + cw[2:3, :] * u
        halo_scr[:, cols] = u[T - SUBLANES:T, :]
        mix_scr[:, j * MIX_GROUP:j * MIX_GROUP + COL_BLOCK] = (
            (cb * conv * _silu(cz)).astype(BF16))

    def rotary(p):
        cosf = cos_ref[...]
        sinf = sin_ref[...]
        heads = []
        for hh in range(2):
            lo = hh * HEAD_DIM
            q = p[:, lo:lo + HEAD_DIM]
            k = p[:, COL_BLOCK + lo:COL_BLOCK + lo + HEAD_DIM]
            v = p[:, 2 * COL_BLOCK + lo:2 * COL_BLOCK + lo + HEAD_DIM]
            z = p[:, 3 * COL_BLOCK + lo:3 * COL_BLOCK + lo + HEAD_DIM]
            q = q * cosf + pltpu.roll(q, HEAD_DIM // 2, 1) * sinf
            k = k * cosf + pltpu.roll(k, HEAD_DIM // 2, 1) * sinf
            heads.append((q, k, v.astype(BF16), z))
        return heads

    def scores_and_kv(pair, heads):
        out = []
        for hh, (q, k, vb, z) in enumerate(heads):
            head = 2 * pair + hh
            st = state_scr[head]
            for c in range(n_chunks):
                rows = slice(c * CHUNK, (c + 1) * CHUNK)
                qc, vc = q[rows], vb[rows]
                kt = k[rows].T
                sc = jnp.dot(qc.astype(BF16), kt.astype(BF16), preferred_element_type=F32)
                lhs = jnp.concatenate(
                    [(sc * decay_ref[head]).astype(BF16), (qc * qw_ref[head]).astype(BF16)],
                    axis=1)
                rhs = jnp.concatenate([vc, st.astype(BF16)], axis=0)
                out.append((head, rows, lhs, rhs, z[rows]))
                kv = jnp.dot((kt * kw_ref[head]).astype(BF16), vc, preferred_element_type=F32)
                st = cdec_ref[head] * st + kv
            state_scr[head] = st
        return out

    def ret_outputs(items):
        return [(head, rows, jnp.dot(lhs, rhs, preferred_element_type=F32), zc)
                for head, rows, lhs, rhs, zc in items]

    def ret_epilogue(items):
        for head, rows, o, zc in items:
            rn = o * lax.rsqrt(jnp.mean(o * o, axis=-1, keepdims=True) + NORM_EPS)
            lo = (head // 2) * MIX_GROUP + COL_BLOCK + (head % 2) * HEAD_DIM
            mix_scr[rows, lo:lo + HEAD_DIM] = (rn * _silu(zc)).astype(BF16)

    def out_proj(j0, j1):
        rhs = jnp.concatenate(
            [wout_ref[base + j * COL_BLOCK:base + (j + 1) * COL_BLOCK, :]
             for j in range(j0, j1) for base in (0, D_CONV)], axis=0)
        return jnp.dot(mix_scr[:, j0 * MIX_GROUP:j1 * MIX_GROUP], rhs,
                       preferred_element_type=F32)

    def weight_copies():
        early = 2 * COL_BLOCK
        spans = [(RET_BASE + kind * D_CONV, early) for kind in range(4)]
        spans += [(RET_BASE + kind * D_CONV + early, D_CONV - early) for kind in range(4)]
        spans += [(0, RET_BASE)]
        copies = [pltpu.make_async_copy(w_hbm.at[:, pl.ds(lo, n)], w_ref.at[:, pl.ds(lo, n)],
                                        w_sem.at[i]) for i, (lo, n) in enumerate(spans)]
        copies.append(pltpu.make_async_copy(wout_hbm, wout_ref, w_sem.at[len(spans)]))
        return copies

    @pl.when(g == 0)
    def _():
        copies = weight_copies()
        for c in copies:
            c.start()
        pre_norm(x_ref)
        for c in copies[:4]:
            c.wait()
        pa_scr[0] = in_proj(RET_BASE, 0)
        pa_scr[1] = in_proj(RET_BASE, 1)
        for c in copies[4:]:
            c.wait()

    @pl.when(g % tiles_per_seq == 0)
    def _():
        state_scr[...] = jnp.zeros_like(state_scr)
        halo_scr[...] = jnp.zeros_like(halo_scr)

    wnext_bf_ref[...] = wnext_ref[...].astype(BF16)
    woutnext_bf_ref[...] = woutnext_ref[...].astype(BF16)

    pa = [pa_scr[0], pa_scr[1], in_proj(RET_BASE, 2)]
    sk = [scores_and_kv(0, rotary(pa[0]))]
    pa.append(in_proj(RET_BASE, 3))
    ro = [ret_outputs(sk[0])]
    sk.append(scores_and_kv(1, rotary(pa[1])))
    pb = in_proj(0, 0)
    ret_epilogue(ro[0])
    ro.append(ret_outputs(sk[1]))
    sk.append(scores_and_kv(2, rotary(pa[2])))
    conv_epilogue(0, pb)
    pb = in_proj(0, 1)
    ret_epilogue(ro[1])
    ro.append(ret_outputs(sk[2]))
    sk.append(scores_and_kv(3, rotary(pa[3])))
    conv_epilogue(1, pb)
    pb = in_proj(0, 2)
    ret_epilogue(ro[2])
    ro.append(ret_outputs(sk[3]))
    conv_epilogue(2, pb)
    pb = in_proj(0, 3)
    ret_epilogue(ro[3])
    out_first = out_proj(0, 2)
    pre_norm(xnext_ref)
    pa_scr[0] = in_proj(RET_BASE, 0)
    conv_epilogue(3, pb)
    out = out_first + out_proj(2, 4)
    pa_scr[1] = in_proj(RET_BASE, 1)
    gain = gpost_ref[layer:layer + 1, :]
    y = out * lax.rsqrt(jnp.mean(out * out, axis=-1, keepdims=True) + NORM_EPS) * gain
    o_ref[...] = x_ref[...] + y


def _layer_call(layer, x, cosf, sinf, gpre, w, cw, wout, gpost, decay, qw, kw, cdec,
                w_in_f32, w_out_f32, seq_len):
    n, d = x.shape
    depth = w_in_f32.shape[0]
    n_tiles = n // SEQ_TILE
    w_cols = w.shape[1] // n_tiles
    wout_rows = wout.shape[0] // n_tiles
    assert w_cols % HEAD_DIM == 0 and wout_rows % (2 * SUBLANES) == 0
    nxt = min(layer + 1, depth - 1)
    cur_tile = lambda g: (g, 0)
    next_tile = lambda g: (jnp.minimum(g + 1, n_tiles - 1), 0)
    const2 = lambda g: (0, 0)
    const3 = lambda g: (0, 0, 0)
    this_layer = lambda g: (layer, 0, 0)
    once = pl.Buffered(1)
    in_specs = [
        pl.BlockSpec((SEQ_TILE, d), cur_tile),
        pl.BlockSpec((SEQ_TILE, d), next_tile),
        pl.BlockSpec((SEQ_TILE, HEAD_DIM), cur_tile),
        pl.BlockSpec((SEQ_TILE, HEAD_DIM), cur_tile),
        pl.BlockSpec(gpre.shape, const2, pipeline_mode=once),
        pl.BlockSpec(memory_space=pl.ANY),
        pl.BlockSpec((None,) + cw.shape[1:], this_layer, pipeline_mode=once),
        pl.BlockSpec(memory_space=pl.ANY),
        pl.BlockSpec(gpost.shape, const2, pipeline_mode=once),
        pl.BlockSpec(decay.shape, const3, pipeline_mode=once),
        pl.BlockSpec(qw.shape, const3, pipeline_mode=once),
        pl.BlockSpec(kw.shape, const3, pipeline_mode=once),
        pl.BlockSpec(cdec.shape, const3, pipeline_mode=once),
        pl.BlockSpec((None, w.shape[0], w_cols), lambda g: (nxt, 0, g)),
        pl.BlockSpec((None, wout_rows, wout.shape[1]), lambda g: (nxt, g, 0)),
    ]
    out_specs = [
        pl.BlockSpec((SEQ_TILE, d), cur_tile),
        pl.BlockSpec((w.shape[0], w_cols), lambda g: (0, g)),
        pl.BlockSpec((wout_rows, wout.shape[1]), lambda g: (g, 0)),
    ]
    out_shape = [
        jax.ShapeDtypeStruct(x.shape, x.dtype),
        jax.ShapeDtypeStruct(w.shape, BF16),
        jax.ShapeDtypeStruct(wout.shape, BF16),
    ]
    return pl.pallas_call(
        functools.partial(_layer_kernel, layer=layer, tiles_per_seq=seq_len // SEQ_TILE),
        grid=(n_tiles,),
        in_specs=in_specs,
        out_specs=out_specs,
        out_shape=out_shape,
        scratch_shapes=[
            pltpu.VMEM((SEQ_TILE, d), BF16),
            pltpu.VMEM((2, SEQ_TILE, 4 * COL_BLOCK), F32),
            pltpu.VMEM((SEQ_TILE, D_MIX), BF16),
            pltpu.VMEM((RET_HEADS, HEAD_DIM, HEAD_DIM), F32),
            pltpu.VMEM((SUBLANES, D_CONV), F32),
            pltpu.VMEM(w.shape, BF16),
            pltpu.VMEM(wout.shape, BF16),
            pltpu.SemaphoreType.DMA((N_WEIGHT_COPIES,)),
        ],
        compiler_params=pltpu.CompilerParams(
            dimension_semantics=("arbitrary",),
            vmem_limit_bytes=VMEM_LIMIT_BYTES),
        name="hybrid_layer",
    )(x, x, cosf, sinf, gpre, w, cw, wout, gpost, decay, qw, kw, cdec, w_in_f32, w_out_f32)


def _retention_tables():
    f32 = np.float32
    log_gamma = np.log1p(-np.exp2(-5.0 - np.arange(RET_HEADS, dtype=f32))).astype(f32)
    scale = f32(HEAD_DIM ** -0.5)
    idx = np.arange(CHUNK, dtype=f32)
    diff = idx[:, None] - idx[None, :]
    causal = diff >= 0
    decay = np.where(causal[None],
                     np.exp(log_gamma[:, None, None] * np.where(causal, diff, f32(0))[None]),
                     f32(0)).astype(f32)
    k_w = np.exp((CHUNK - 1 - idx)[None, :] * log_gamma[:, None]).astype(f32)
    q_w = np.exp((idx + 1)[None, :] * log_gamma[:, None]).astype(f32)
    cdec = np.exp(f32(CHUNK) * log_gamma).astype(f32)
    q_w_rows = np.broadcast_to(q_w[:, :, None], (RET_HEADS, CHUNK, HEAD_DIM))
    k_w_t = np.broadcast_to((k_w * scale)[:, None, :], (RET_HEADS, HEAD_DIM, CHUNK))
    cdec_rows = np.broadcast_to(cdec[:, None, None], (RET_HEADS, 1, HEAD_DIM))
    return tuple(jnp.asarray(np.ascontiguousarray(t))
                 for t in (decay * scale, q_w_rows, k_w_t, cdec_rows))


def kernel(x, positions, pre_norm, w_in, conv_w, w_out, post_norm):
    depth = w_in.shape[0]
    b, s, d = x.shape
    assert d == D_MODEL and w_in.shape[1:] == (D_MODEL, 4 * D_CONV + 4 * D_RET)
    assert conv_w.shape[1:] == (CONV_K, D_CONV) and w_out.shape[1:] == (D_MIX, D_MODEL)
    assert s % SEQ_TILE == 0 and (b * s) % ROPE_TILE == 0
    half = HEAD_DIM // 2
    inv_freq = 1.0 / (ROPE_BASE ** (jnp.arange(half, dtype=F32) / half))
    inv_freq2 = jnp.concatenate([inv_freq, inv_freq]).reshape(1, HEAD_DIM)
    cosf, sinf, w_bf, wout_bf = _rope_tables(positions, inv_freq2, w_in, w_out)
    decay, qw, kw, cdec = _retention_tables()
    x = x.reshape(b * s, d)
    for layer in range(depth):
        x, w_bf, wout_bf = _layer_call(
            layer, x, cosf, sinf, pre_norm, w_bf, conv_w, wout_bf, post_norm, decay, qw, kw, cdec,
            w_in, w_out, s)
    return x.reshape(b, s, d)
```

```python
import functools

import jax
import jax.numpy as jnp
import numpy as np
from jax import lax
from jax.experimental import pallas as pl
from jax.experimental.pallas import tpu as pltpu

D_MODEL = 1024
D_MIX = 2 * D_MODEL
D_CONV = D_MIX // 2
D_RET = D_MIX - D_CONV
CONV_K = 3
RET_HEADS = 8
HEAD_DIM = D_RET // RET_HEADS
ROPE_BASE = 10000.0
NORM_EPS = 1e-6

CHUNK = 128
SEQ_TILE = 512
COL_BLOCK = 256
N_BLOCKS = D_CONV // COL_BLOCK
RET_BASE = 4 * D_CONV
MIX_GROUP = 2 * COL_BLOCK
ROPE_TILE = 2048
SUBLANES = 8
N_WEIGHT_COPIES = 10
VMEM_LIMIT_BYTES = 56 * 1024 * 1024

F32 = jnp.float32
BF16 = jnp.bfloat16


def _silu(x):
    h = 0.5 * x
    return h + h * jnp.tanh(h)


def _rope_kernel(pos_ref, invf_ref, w_ref, wout_ref, cos_ref, sin_ref, w_bf_ref, wout_bf_ref):
    w_bf_ref[...] = w_ref[...].astype(BF16)
    wout_bf_ref[...] = wout_ref[...].astype(BF16)

    half_t = pos_ref.shape[0] // 2
    half_d = HEAD_DIM // 2
    first = lax.broadcasted_iota(jnp.int32, (half_t, HEAD_DIM), 1) < half_d
    pos2 = jnp.where(first, pos_ref[0:half_t, :], pos_ref[half_t:2 * half_t, :])
    ang = pos2.astype(F32) * invf_ref[...]
    c = jnp.cos(ang)
    s = jnp.sin(ang)
    c_sw = pltpu.roll(c, half_d, 1)
    s_sw = pltpu.roll(s, half_d, 1)
    cos_ref[0:half_t, :] = jnp.where(first, c, c_sw)
    cos_ref[half_t:2 * half_t, :] = jnp.where(first, c_sw, c)
    sin_ref[0:half_t, :] = jnp.where(first, -s, s_sw)
    sin_ref[half_t:2 * half_t, :] = jnp.where(first, -s_sw, s)


def _rope_tables(positions, inv_freq2, w_in, w_out):
    n = positions.size
    pos = positions.reshape(n, 1)
    steps = n // ROPE_TILE
    _, d_in, n_cols = w_in.shape
    _, n_rows, d_out = w_out.shape
    w_cols, wout_rows = n_cols // steps, n_rows // steps
    assert w_cols % HEAD_DIM == 0 and wout_rows % (2 * SUBLANES) == 0
    return pl.pallas_call(
        _rope_kernel,
        grid=(steps,),
        in_specs=[pl.BlockSpec((ROPE_TILE, 1), lambda i: (i, 0)),
                  pl.BlockSpec((1, HEAD_DIM), lambda i: (0, 0)),
                  pl.BlockSpec((None, d_in, w_cols), lambda i: (0, 0, i)),
                  pl.BlockSpec((None, wout_rows, d_out), lambda i: (0, i, 0))],
        out_specs=[pl.BlockSpec((ROPE_TILE, HEAD_DIM), lambda i: (i, 0)),
                   pl.BlockSpec((ROPE_TILE, HEAD_DIM), lambda i: (i, 0)),
                   pl.BlockSpec((d_in, w_cols), lambda i: (0, i)),
                   pl.BlockSpec((wout_rows, d_out), lambda i: (i, 0))],
        out_shape=[jax.ShapeDtypeStruct((n, HEAD_DIM), F32)] * 2 + [
            jax.ShapeDtypeStruct((d_in, n_cols), BF16),
            jax.ShapeDtypeStruct((n_rows, d_out), BF16)],
        name="rope_tables",
    )(pos, inv_freq2, w_in, w_out)


def _layer_kernel(x_ref, xnext_ref, cos_ref, sin_ref, gpre_ref, w_hbm, cw_ref, wout_hbm,
                  gpost_ref, decay_ref, qw_ref, kw_ref, cdec_ref, wnext_ref, woutnext_ref,
                  o_ref, wnext_bf_ref, woutnext_bf_ref,
                  h_scr, pa_scr, mix_scr, state_scr, halo_scr, w_ref, wout_ref, w_sem,
                  *, layer, tiles_per_seq):
    T = SEQ_TILE
    n_chunks = T // CHUNK
    g = pl.program_id(0)

    def pre_norm(x_in_ref):
        x = x_in_ref[...]
        gain = gpre_ref[layer:layer + 1, :]
        h = x * lax.rsqrt(jnp.mean(x * x, axis=-1, keepdims=True) + NORM_EPS) * gain
        h_scr[...] = h.astype(BF16)

    def in_proj(base, blk):
        rhs = jnp.concatenate(
            [w_ref[:, base + kind * D_CONV + blk * COL_BLOCK:
                   base + kind * D_CONV + (blk + 1) * COL_BLOCK] for kind in range(4)], axis=1)
        return jnp.dot(h_scr[...], rhs, preferred_element_type=F32)

    def conv_epilogue(j, p):
        cols = slice(j * COL_BLOCK, (j + 1) * COL_BLOCK)
        cb = p[:, 0:COL_BLOCK]
        cc = p[:, COL_BLOCK:2 * COL_BLOCK]
        cx = p[:, 2 * COL_BLOCK:3 * COL_BLOCK]
        cz = p[:, 3 * COL_BLOCK:4 * COL_BLOCK]
        u = cc * cx
        prev = halo_scr[:, cols]
        row = lax.broadcasted_iota(jnp.int32, (SUBLANES, COL_BLOCK), 0)
        u1 = pltpu.roll(u, 1, 0)
        u2 = pltpu.roll(u, 2, 0)
        top1 = jnp.where(row == 0, prev[7:8, :], u1[0:SUBLANES])
        top2 = jnp.where(row == 0, prev[6:7, :],
                         jnp.where(row == 1, prev[7:8, :], u2[0:SUBLANES]))
        u1 = jnp.concatenate([top1, u1[SUBLANES:]], axis=0)
        u2 = jnp.concatenate([top2, u2[SUBLANES:]], axis=0)
        cw = cw_ref[:, cols]
        conv = cw[0:1, :] * u2 + cw[1:2, :] * u1 + cw[2:3, :] * u
        halo_scr[:, cols] = u[T - SUBLANES:T, :]
        mix_scr[:, j * MIX_GROUP:j * MIX_GROUP + COL_BLOCK] = (
            (cb * conv * _silu(cz)).astype(BF16))

    def rotary(p):
        cosf = cos_ref[...]
        sinf = sin_ref[...]
        heads = []
        for hh in range(2):
            lo = hh * HEAD_DIM
            q = p[:, lo:lo + HEAD_DIM]
            k = p[:, COL_BLOCK + lo:COL_BLOCK + lo + HEAD_DIM]
            v = p[:, 2 * COL_BLOCK + lo:2 * COL_BLOCK + lo + HEAD_DIM]
            z = p[:, 3 * COL_BLOCK + lo:3 * COL_BLOCK + lo + HEAD_DIM]
            q = q * cosf + pltpu.roll(q, HEAD_DIM // 2, 1) * sinf
            k = k * cosf + pltpu.roll(k, HEAD_DIM // 2, 1) * sinf
            heads.append((q, k, v.astype(BF16), z))
        return heads

    def scores_and_kv(pair, heads):
        out = []
        for hh, (q, k, vb, z) in enumerate(heads):
            head = 2 * pair + hh
            q4 = q.reshape(n_chunks, CHUNK, HEAD_DIM)
            v4 = vb.reshape(n_chunks, CHUNK, HEAD_DIM)
            kt4 = jnp.swapaxes(k.reshape(n_chunks, CHUNK, HEAD_DIM), 1, 2)
            sc4 = jnp.einsum("cqd,cdk->cqk", q4.astype(BF16), kt4.astype(BF16),
                             preferred_element_type=F32)
            kv4 = jnp.einsum("cdt,cte->cde", (kt4 * kw_ref[head][None]).astype(BF16), v4,
                             preferred_element_type=F32)
            lhs4 = jnp.concatenate(
                [(sc4 * decay_ref[head][None]).astype(BF16),
                 (q4 * qw_ref[head][None]).astype(BF16)], axis=2)
            st = state_scr[head]
            states = []
            for c in range(n_chunks):
                states.append(st.astype(BF16))
                st = cdec_ref[head] * st + kv4[c]
            state_scr[head] = st
            rhs4 = jnp.concatenate([v4, jnp.stack(states)], axis=1)
            out.append((head, lhs4, rhs4, z))
        return out

    def ret_outputs(items):
        return [(head, jnp.einsum("cqk,cke->cqe", lhs4, rhs4, preferred_element_type=F32)
                 .reshape(T, HEAD_DIM), z) for head, lhs4, rhs4, z in items]

    def ret_epilogue(items):
        for head, o, z in items:
            rn = o * lax.rsqrt(jnp.mean(o * o, axis=-1, keepdims=True) + NORM_EPS)
            lo = (head // 2) * MIX_GROUP + COL_BLOCK + (head % 2) * HEAD_DIM
            mix_scr[:, lo:lo + HEAD_DIM] = (rn * _silu(z)).astype(BF16)

    def out_proj(j0, j1):
        rhs = jnp.concatenate(
            [wout_ref[base + j * COL_BLOCK:base + (j + 1) * COL_BLOCK, :]
             for j in range(j0, j1) for base in (0, D_CONV)], axis=0)
        return jnp.dot(mix_scr[:, j0 * MIX_GROUP:j1 * MIX_GROUP], rhs,
                       preferred_element_type=F32)

    def weight_copies():
        early = 2 * COL_BLOCK
        spans = [(RET_BASE + kind * D_CONV, early) for kind in range(4)]
        spans += [(RET_BASE + kind * D_CONV + early, D_CONV - early) for kind in range(4)]
        spans += [(0, RET_BASE)]
        copies = [pltpu.make_async_copy(w_hbm.at[:, pl.ds(lo, n)], w_ref.at[:, pl.ds(lo, n)],
                                        w_sem.at[i]) for i, (lo, n) in enumerate(spans)]
        copies.append(pltpu.make_async_copy(wout_hbm, wout_ref, w_sem.at[len(spans)]))
        return copies

    @pl.when(g == 0)
    def _():
        copies = weight_copies()
        for c in copies:
            c.start()
        pre_norm(x_ref)
        for c in copies[:4]:
            c.wait()
        pa_scr[0] = in_proj(RET_BASE, 0)
        pa_scr[1] = in_proj(RET_BASE, 1)
        for c in copies[4:]:
            c.wait()

    @pl.when(g % tiles_per_seq == 0)
    def _():
        state_scr[...] = jnp.zeros_like(state_scr)
        halo_scr[...] = jnp.zeros_like(halo_scr)

    wnext_bf_ref[...] = wnext_ref[...].astype(BF16)
    woutnext_bf_ref[...] = woutnext_ref[...].astype(BF16)

    pa = [pa_scr[0], pa_scr[1], in_proj(RET_BASE, 2)]
    sk = [scores_and_kv(0, rotary(pa[0]))]
    pa.append(in_proj(RET_BASE, 3))
    ro = [ret_outputs(sk[0])]
    sk.append(scores_and_kv(1, rotary(pa[1])))
    pb = in_proj(0, 0)
    ret_epilogue(ro[0])
    ro.append(ret_outputs(sk[1]))
    sk.append(scores_and_kv(2, rotary(pa[2])))
    conv_epilogue(0, pb)
    pb = in_proj(0, 1)
    ret_epilogue(ro[1])
    ro.append(ret_outputs(sk[2]))
    sk.append(scores_and_kv(3, rotary(pa[3])))
    conv_epilogue(1, pb)
    pb = in_proj(0, 2)
    ret_epilogue(ro[2])
    ro.append(ret_outputs(sk[3]))
    conv_epilogue(2, pb)
    pb = in_proj(0, 3)
    ret_epilogue(ro[3])
    out_first = out_proj(0, 2)
    pre_norm(xnext_ref)
    pa_scr[0] = in_proj(RET_BASE, 0)
    conv_epilogue(3, pb)
    out = out_first + out_proj(2, 4)
    pa_scr[1] = in_proj(RET_BASE, 1)
    gain = gpost_ref[layer:layer + 1, :]
    y = out * lax.rsqrt(jnp.mean(out * out, axis=-1, keepdims=True) + NORM_EPS) * gain
    o_ref[...] = x_ref[...] + y


def _layer_call(layer, x, cosf, sinf, gpre, w, cw, wout, gpost, decay, qw, kw, cdec,
                w_in_f32, w_out_f32, seq_len):
    n, d = x.shape
    depth = w_in_f32.shape[0]
    n_tiles = n // SEQ_TILE
    w_cols = w.shape[1] // n_tiles
    wout_rows = wout.shape[0] // n_tiles
    assert w_cols % HEAD_DIM == 0 and wout_rows % (2 * SUBLANES) == 0
    nxt = min(layer + 1, depth - 1)
    cur_tile = lambda g: (g, 0)
    next_tile = lambda g: (jnp.minimum(g + 1, n_tiles - 1), 0)
    const2 = lambda g: (0, 0)
    const3 = lambda g: (0, 0, 0)
    this_layer = lambda g: (layer, 0, 0)
    once = pl.Buffered(1)
    in_specs = [
        pl.BlockSpec((SEQ_TILE, d), cur_tile),
        pl.BlockSpec((SEQ_TILE, d), next_tile),
        pl.BlockSpec((SEQ_TILE, HEAD_DIM), cur_tile),
        pl.BlockSpec((SEQ_TILE, HEAD_DIM), cur_tile),
        pl.BlockSpec(gpre.shape, const2, pipeline_mode=once),
        pl.BlockSpec(memory_space=pl.ANY),
        pl.BlockSpec((None,) + cw.shape[1:], this_layer, pipeline_mode=once),
        pl.BlockSpec(memory_space=pl.ANY),
        pl.BlockSpec(gpost.shape, const2, pipeline_mode=once),
        pl.BlockSpec(decay.shape, const3, pipeline_mode=once),
        pl.BlockSpec(qw.shape, const3, pipeline_mode=once),
        pl.BlockSpec(kw.shape, const3, pipeline_mode=once),
        pl.BlockSpec(cdec.shape, const3, pipeline_mode=once),
        pl.BlockSpec((None, w.shape[0], w_cols), lambda g: (nxt, 0, g)),
        pl.BlockSpec((None, wout_rows, wout.shape[1]), lambda g: (nxt, g, 0)),
    ]
    out_specs = [
        pl.BlockSpec((SEQ_TILE, d), cur_tile),
        pl.BlockSpec((w.shape[0], w_cols), lambda g: (0, g)),
        pl.BlockSpec((wout_rows, wout.shape[1]), lambda g: (g, 0)),
    ]
    out_shape = [
        jax.ShapeDtypeStruct(x.shape, x.dtype),
        jax.ShapeDtypeStruct(w.shape, BF16),
        jax.ShapeDtypeStruct(wout.shape, BF16),
    ]
    return pl.pallas_call(
        functools.partial(_layer_kernel, layer=layer, tiles_per_seq=seq_len // SEQ_TILE),
        grid=(n_tiles,),
        in_specs=in_specs,
        out_specs=out_specs,
        out_shape=out_shape,
        scratch_shapes=[
            pltpu.VMEM((SEQ_TILE, d), BF16),
            pltpu.VMEM((2, SEQ_TILE, 4 * COL_BLOCK), F32),
            pltpu.VMEM((SEQ_TILE, D_MIX), BF16),
            pltpu.VMEM((RET_HEADS, HEAD_DIM, HEAD_DIM), F32),
            pltpu.VMEM((SUBLANES, D_CONV), F32),
            pltpu.VMEM(w.shape, BF16),
            pltpu.VMEM(wout.shape, BF16),
            pltpu.SemaphoreType.DMA((N_WEIGHT_COPIES,)),
        ],
        compiler_params=pltpu.CompilerParams(
            dimension_semantics=("arbitrary",),
            vmem_limit_bytes=VMEM_LIMIT_BYTES),
        name="hybrid_layer",
    )(x, x, cosf, sinf, gpre, w, cw, wout, gpost, decay, qw, kw, cdec, w_in_f32, w_out_f32)


def _retention_tables():
    f32 = np.float32
    log_gamma = np.log1p(-np.exp2(-5.0 - np.arange(RET_HEADS, dtype=f32))).astype(f32)
    scale = f32(HEAD_DIM ** -0.5)
    idx = np.arange(CHUNK, dtype=f32)
    diff = idx[:, None] - idx[None, :]
    causal = diff >= 0
    decay = np.where(causal[None],
                     np.exp(log_gamma[:, None, None] * np.where(causal, diff, f32(0))[None]),
                     f32(0)).astype(f32)
    k_w = np.exp((CHUNK - 1 - idx)[None, :] * log_gamma[:, None]).astype(f32)
    q_w = np.exp((idx + 1)[None, :] * log_gamma[:, None]).astype(f32)
    cdec = np.exp(f32(CHUNK) * log_gamma).astype(f32)
    q_w_rows = np.broadcast_to(q_w[:, :, None], (RET_HEADS, CHUNK, HEAD_DIM))
    k_w_t = np.broadcast_to((k_w * scale)[:, None, :], (RET_HEADS, HEAD_DIM, CHUNK))
    cdec_rows = np.broadcast_to(cdec[:, None, None], (RET_HEADS, 1, HEAD_DIM))
    return tuple(jnp.asarray(np.ascontiguousarray(t))
                 for t in (decay * scale, q_w_rows, k_w_t, cdec_rows))


def kernel(x, positions, pre_norm, w_in, conv_w, w_out, post_norm):
    depth = w_in.shape[0]
    b, s, d = x.shape
    assert d == D_MODEL and w_in.shape[1:] == (D_MODEL, 4 * D_CONV + 4 * D_RET)
    assert conv_w.shape[1:] == (CONV_K, D_CONV) and w_out.shape[1:] == (D_MIX, D_MODEL)
    assert s % SEQ_TILE == 0 and (b * s) % ROPE_TILE == 0
    half = HEAD_DIM // 2
    inv_freq = 1.0 / (ROPE_BASE ** (jnp.arange(half, dtype=F32) / half))
    inv_freq2 = jnp.concatenate([inv_freq, inv_freq]).reshape(1, HEAD_DIM)
    cosf, sinf, w_bf, wout_bf = _rope_tables(positions, inv_freq2, w_in, w_out)
    decay, qw, kw, cdec = _retention_tables()
    x = x.reshape(b * s, d)
    for layer in range(depth):
        x, w_bf, wout_bf = _layer_call(
            layer, x, cosf, sinf, pre_norm, w_bf, conv_w, wout_bf, post_norm, decay, qw, kw, cdec,
            w_in, w_out, s)
    return x.reshape(b, s, d)
```

```python
import functools

import jax
import jax.numpy as jnp
import numpy as np
from jax import lax
from jax.experimental import pallas as pl
from jax.experimental.pallas import tpu as pltpu

D_MODEL = 1024
D_MIX = 2 * D_MODEL
D_CONV = D_MIX // 2
D_RET = D_MIX - D_CONV
CONV_K = 3
RET_HEADS = 8
HEAD_DIM = D_RET // RET_HEADS
ROPE_BASE = 10000.0
NORM_EPS = 1e-6

CHUNK = 128
SEQ_TILE = 512
COL_BLOCK = 256
N_BLOCKS = D_CONV // COL_BLOCK
RET_BASE = 4 * D_CONV
MIX_GROUP = 2 * COL_BLOCK
ROPE_TILE = 2048
SUBLANES = 8
N_WEIGHT_COPIES = 10
VMEM_LIMIT_BYTES = 56 * 1024 * 1024

F32 = jnp.float32
BF16 = jnp.bfloat16


def _silu(x):
    h = 0.5 * x
    return h + h * jnp.tanh(h)


def _rope_kernel(pos_ref, invf_ref, w_ref, wout_ref, cos_ref, sin_ref, w_bf_ref, wout_bf_ref):
    w_bf_ref[...] = w_ref[...].astype(BF16)
    wout_bf_ref[...] = wout_ref[...].astype(BF16)

    half_t = pos_ref.shape[0] // 2
    half_d = HEAD_DIM // 2
    first = lax.broadcasted_iota(jnp.int32, (half_t, HEAD_DIM), 1) < half_d
    pos2 = jnp.where(first, pos_ref[0:half_t, :], pos_ref[half_t:2 * half_t, :])
    ang = pos2.astype(F32) * invf_ref[...]
    c = jnp.cos(ang)
    s = jnp.sin(ang)
    c_sw = pltpu.roll(c, half_d, 1)
    s_sw = pltpu.roll(s, half_d, 1)
    cos_ref[0:half_t, :] = jnp.where(first, c, c_sw)
    cos_ref[half_t:2 * half_t, :] = jnp.where(first, c_sw, c)
    sin_ref[0:half_t, :] = jnp.where(first, -s, s_sw)
    sin_ref[half_t:2 * half_t, :] = jnp.where(first, -s_sw, s)


def _rope_tables(positions, inv_freq2, w_in, w_out):
    n = positions.size
    pos = positions.reshape(n, 1)
    steps = n // ROPE_TILE
    _, d_in, n_cols = w_in.shape
    _, n_rows, d_out = w_out.shape
    w_cols, wout_rows = n_cols // steps, n_rows // steps
    assert w_cols % HEAD_DIM == 0 and wout_rows % (2 * SUBLANES) == 0
    return pl.pallas_call(
        _rope_kernel,
        grid=(steps,),
        in_specs=[pl.BlockSpec((ROPE_TILE, 1), lambda i: (i, 0)),
                  pl.BlockSpec((1, HEAD_DIM), lambda i: (0, 0)),
                  pl.BlockSpec((None, d_in, w_cols), lambda i: (0, 0, i)),
                  pl.BlockSpec((None, wout_rows, d_out), lambda i: (0, i, 0))],
        out_specs=[pl.BlockSpec((ROPE_TILE, HEAD_DIM), lambda i: (i, 0)),
                   pl.BlockSpec((ROPE_TILE, HEAD_DIM), lambda i: (i, 0)),
                   pl.BlockSpec((d_in, w_cols), lambda i: (0, i)),
                   pl.BlockSpec((wout_rows, d_out), lambda i: (i, 0))],
        out_shape=[jax.ShapeDtypeStruct((n, HEAD_DIM), F32)] * 2 + [
            jax.ShapeDtypeStruct((d_in, n_cols), BF16),
            jax.ShapeDtypeStruct((n_rows, d_out), BF16)],
        name="rope_tables",
    )(pos, inv_freq2, w_in, w_out)


def _layer_kernel(x_ref, xnext_ref, cos_ref, sin_ref, gpre_ref, w_hbm, cw_ref, wout_hbm,
                  gpost_ref, decay_ref, qw_ref, kw_ref, cdec_ref, wnext_ref, woutnext_ref,
                  o_ref, wnext_bf_ref, woutnext_bf_ref,
                  h_scr, pa_scr, mix_scr, state_scr, halo_scr, w_ref, wout_ref, w_sem,
                  *, layer, tiles_per_seq):
    T = SEQ_TILE
    n_chunks = T // CHUNK
    g = pl.program_id(0)

    def pre_norm(x_in_ref):
        x = x_in_ref[...]
        gain = gpre_ref[layer:layer + 1, :]
        h = x * lax.rsqrt(jnp.mean(x * x, axis=-1, keepdims=True) + NORM_EPS) * gain
        h_scr[...] = h.astype(BF16)

    def in_proj(base, blk):
        rhs = jnp.concatenate(
            [w_ref[:, base + kind * D_CONV + blk * COL_BLOCK:
                   base + kind * D_CONV + (blk + 1) * COL_BLOCK] for kind in range(4)], axis=1)
        return jnp.dot(h_scr[...], rhs, preferred_element_type=F32)

    def conv_epilogue(j, p):
        cols = slice(j * COL_BLOCK, (j + 1) * COL_BLOCK)
        cb = p[:, 0:COL_BLOCK]
        cc = p[:, COL_BLOCK:2 * COL_BLOCK]
        cx = p[:, 2 * COL_BLOCK:3 * COL_BLOCK]
        cz = p[:, 3 * COL_BLOCK:4 * COL_BLOCK]
        u = cc * cx
        prev = halo_scr[:, cols]
        row = lax.broadcasted_iota(jnp.int32, (SUBLANES, COL_BLOCK), 0)
        u1 = pltpu.roll(u, 1, 0)
        u2 = pltpu.roll(u, 2, 0)
        top1 = jnp.where(row == 0, prev[7:8, :], u1[0:SUBLANES])
        top2 = jnp.where(row == 0, prev[6:7, :],
                         jnp.where(row == 1, prev[7:8, :], u2[0:SUBLANES]))
        u1 = jnp.concatenate([top1, u1[SUBLANES:]], axis=0)
        u2 = jnp.concatenate([top2, u2[SUBLANES:]], axis=0)
        cw = cw_ref[:, cols]
        conv = cw[0:1, :] * u2 + cw[1:2, :] * u1 + cw[2:3, :] * u
        halo_scr[:, cols] = u[T - SUBLANES:T, :]
        mix_scr[:, j * MIX_GROUP:j * MIX_GROUP + COL_BLOCK] = (
            (cb * conv * _silu(cz)).astype(BF16))

    def rotary(p):
        cosf = cos_ref[...]
        sinf = sin_ref[...]
        heads = []
        for hh in range(2):
            lo = hh * HEAD_DIM
            q = p[:, lo:lo + HEAD_DIM]
            k = p[:, COL_BLOCK + lo:COL_BLOCK + lo + HEAD_DIM]
            v = p[:, 2 * COL_BLOCK + lo:2 * COL_BLOCK + lo + HEAD_DIM]
            z = p[:, 3 * COL_BLOCK + lo:3 * COL_BLOCK + lo + HEAD_DIM]
            q = q * cosf + pltpu.roll(q, HEAD_DIM // 2, 1) * sinf
            k = k * cosf + pltpu.roll(k, HEAD_DIM // 2, 1) * sinf
            heads.append((q, k, v.astype(BF16), z))
        return heads

    def scores_and_kv(pair, heads):
        out = []
        for hh, (q, k, vb, z) in enumerate(heads):
            head = 2 * pair + hh
            st = state_scr[head]
            for c in range(n_chunks):
                rows = slice(c * CHUNK, (c + 1) * CHUNK)
                qc, vc = q[rows], vb[rows]
                kt = k[rows].T
                sc = lax.dot_general(qc.astype(BF16), k[rows].astype(BF16),
                                     (((1,), (1,)), ((), ())), preferred_element_type=F32)
                lhs = jnp.concatenate(
                    [(sc * decay_ref[head]).astype(BF16), (qc * qw_ref[head]).astype(BF16)],
                    axis=1)
                rhs = jnp.concatenate([vc, st.astype(BF16)], axis=0)
                out.append((head, rows, lhs, rhs, z[rows]))
                kv = jnp.dot((kt * kw_ref[head]).astype(BF16), vc, preferred_element_type=F32)
                st = cdec_ref[head] * st + kv
            state_scr[head] = st
        return out

    def ret_outputs(items):
        return [(head, rows, jnp.dot(lhs, rhs, preferred_element_type=F32), zc)
                for head, rows, lhs, rhs, zc in items]

    def ret_epilogue(items):
        for head, rows, o, zc in items:
            rn = o * lax.rsqrt(jnp.mean(o * o, axis=-1, keepdims=True) + NORM_EPS)
            lo = (head // 2) * MIX_GROUP + COL_BLOCK + (head % 2) * HEAD_DIM
            mix_scr[rows, lo:lo + HEAD_DIM] = (rn * _silu(zc)).astype(BF16)

    def out_proj(j0, j1):
        rhs = jnp.concatenate(
            [wout_ref[base + j * COL_BLOCK:base + (j + 1) * COL_BLOCK, :]
             for j in range(j0, j1) for base in (0, D_CONV)], axis=0)
        return jnp.dot(mix_scr[:, j0 * MIX_GROUP:j1 * MIX_GROUP], rhs,
                       preferred_element_type=F32)

    def weight_copies():
        early = 2 * COL_BLOCK
        spans = [(RET_BASE + kind * D_CONV, early) for kind in range(4)]
        spans += [(RET_BASE + kind * D_CONV + early, D_CONV - early) for kind in range(4)]
        spans += [(0, RET_BASE)]
        copies = [pltpu.make_async_copy(w_hbm.at[:, pl.ds(lo, n)], w_ref.at[:, pl.ds(lo, n)],
                                        w_sem.at[i]) for i, (lo, n) in enumerate(spans)]
        copies.append(pltpu.make_async_copy(wout_hbm, wout_ref, w_sem.at[len(spans)]))
        return copies

    @pl.when(g == 0)
    def _():
        copies = weight_copies()
        for c in copies:
            c.start()
        pre_norm(x_ref)
        for c in copies[:4]:
            c.wait()
        pa_scr[0] = in_proj(RET_BASE, 0)
        pa_scr[1] = in_proj(RET_BASE, 1)
        for c in copies[4:]:
            c.wait()

    @pl.when(g % tiles_per_seq == 0)
    def _():
        state_scr[...] = jnp.zeros_like(state_scr)
        halo_scr[...] = jnp.zeros_like(halo_scr)

    wnext_bf_ref[...] = wnext_ref[...].astype(BF16)
    woutnext_bf_ref[...] = woutnext_ref[...].astype(BF16)

    pa = [pa_scr[0], pa_scr[1], in_proj(RET_BASE, 2)]
    sk = [scores_and_kv(0, rotary(pa[0]))]
    pa.append(in_proj(RET_BASE, 3))
    ro = [ret_outputs(sk[0])]
    sk.append(scores_and_kv(1, rotary(pa[1])))
    pb = in_proj(0, 0)
    ret_epilogue(ro[0])
    ro.append(ret_outputs(sk[1]))
    sk.append(scores_and_kv(2, rotary(pa[2])))
    conv_epilogue(0, pb)
    pb = in_proj(0, 1)
    ret_epilogue(ro[1])
    ro.append(ret_outputs(sk[2]))
    sk.append(scores_and_kv(3, rotary(pa[3])))
    conv_epilogue(1, pb)
    pb = in_proj(0, 2)
    ret_epilogue(ro[2])
    ro.append(ret_outputs(sk[3]))
    conv_epilogue(2, pb)
    pb = in_proj(0, 3)
    ret_epilogue(ro[3])
    out_first = out_proj(0, 2)
    pre_norm(xnext_ref)
    pa_scr[0] = in_proj(RET_BASE, 0)
    conv_epilogue(3, pb)
    out = out_first + out_proj(2, 4)
    pa_scr[1] = in_proj(RET_BASE, 1)
    gain = gpost_ref[layer:layer + 1, :]
    y = out * lax.rsqrt(jnp.mean(out * out, axis=-1, keepdims=True) + NORM_EPS) * gain
    o_ref[...] = x_ref[...] + y


def _layer_call(layer, x, cosf, sinf, gpre, w, cw, wout, gpost, decay, qw, kw, cdec,
                w_in_f32, w_out_f32, seq_len):
    n, d = x.shape
    depth = w_in_f32.shape[0]
    n_tiles = n // SEQ_TILE
    w_cols = w.shape[1] // n_tiles
    wout_rows = wout.shape[0] // n_tiles
    assert w_cols % HEAD_DIM == 0 and wout_rows % (2 * SUBLANES) == 0
    nxt = min(layer + 1, depth - 1)
    cur_tile = lambda g: (g, 0)
    next_tile = lambda g: (jnp.minimum(g + 1, n_tiles - 1), 0)
    const2 = lambda g: (0, 0)
    const3 = lambda g: (0, 0, 0)
    this_layer = lambda g: (layer, 0, 0)
    once = pl.Buffered(1)
    in_specs = [
        pl.BlockSpec((SEQ_TILE, d), cur_tile),
        pl.BlockSpec((SEQ_TILE, d), next_tile),
        pl.BlockSpec((SEQ_TILE, HEAD_DIM), cur_tile),
        pl.BlockSpec((SEQ_TILE, HEAD_DIM), cur_tile),
        pl.BlockSpec(gpre.shape, const2, pipeline_mode=once),
        pl.BlockSpec(memory_space=pl.ANY),
        pl.BlockSpec((None,) + cw.shape[1:], this_layer, pipeline_mode=once),
        pl.BlockSpec(memory_space=pl.ANY),
        pl.BlockSpec(gpost.shape, const2, pipeline_mode=once),
        pl.BlockSpec(decay.shape, const3, pipeline_mode=once),
        pl.BlockSpec(qw.shape, const3, pipeline_mode=once),
        pl.BlockSpec(kw.shape, const3, pipeline_mode=once),
        pl.BlockSpec(cdec.shape, const3, pipeline_mode=once),
        pl.BlockSpec((None, w.shape[0], w_cols), lambda g: (nxt, 0, g)),
        pl.BlockSpec((None, wout_rows, wout.shape[1]), lambda g: (nxt, g, 0)),
    ]
    out_specs = [
        pl.BlockSpec((SEQ_TILE, d), cur_tile),
        pl.BlockSpec((w.shape[0], w_cols), lambda g: (0, g)),
        pl.BlockSpec((wout_rows, wout.shape[1]), lambda g: (g, 0)),
    ]
    out_shape = [
        jax.ShapeDtypeStruct(x.shape, x.dtype),
        jax.ShapeDtypeStruct(w.shape, BF16),
        jax.ShapeDtypeStruct(wout.shape, BF16),
    ]
    return pl.pallas_call(
        functools.partial(_layer_kernel, layer=layer, tiles_per_seq=seq_len // SEQ_TILE),
        grid=(n_tiles,),
        in_specs=in_specs,
        out_specs=out_specs,
        out_shape=out_shape,
        scratch_shapes=[
            pltpu.VMEM((SEQ_TILE, d), BF16),
            pltpu.VMEM((2, SEQ_TILE, 4 * COL_BLOCK), F32),
            pltpu.VMEM((SEQ_TILE, D_MIX), BF16),
            pltpu.VMEM((RET_HEADS, HEAD_DIM, HEAD_DIM), F32),
            pltpu.VMEM((SUBLANES, D_CONV), F32),
            pltpu.VMEM(w.shape, BF16),
            pltpu.VMEM(wout.shape, BF16),
            pltpu.SemaphoreType.DMA((N_WEIGHT_COPIES,)),
        ],
        compiler_params=pltpu.CompilerParams(
            dimension_semantics=("arbitrary",),
            vmem_limit_bytes=VMEM_LIMIT_BYTES),
        name="hybrid_layer",
    )(x, x, cosf, sinf, gpre, w, cw, wout, gpost, decay, qw, kw, cdec, w_in_f32, w_out_f32)


def _retention_tables():
    f32 = np.float32
    log_gamma = np.log1p(-np.exp2(-5.0 - np.arange(RET_HEADS, dtype=f32))).astype(f32)
    scale = f32(HEAD_DIM ** -0.5)
    idx = np.arange(CHUNK, dtype=f32)
    diff = idx[:, None] - idx[None, :]
    causal = diff >= 0
    decay = np.where(causal[None],
                     np.exp(log_gamma[:, None, None] * np.where(causal, diff, f32(0))[None]),
                     f32(0)).astype(f32)
    k_w = np.exp((CHUNK - 1 - idx)[None, :] * log_gamma[:, None]).astype(f32)
    q_w = np.exp((idx + 1)[None, :] * log_gamma[:, None]).astype(f32)
    cdec = np.exp(f32(CHUNK) * log_gamma).astype(f32)
    q_w_rows = np.broadcast_to(q_w[:, :, None], (RET_HEADS, CHUNK, HEAD_DIM))
    k_w_t = np.broadcast_to((k_w * scale)[:, None, :], (RET_HEADS, HEAD_DIM, CHUNK))
    cdec_rows = np.broadcast_to(cdec[:, None, None], (RET_HEADS, 1, HEAD_DIM))
    return tuple(jnp.asarray(np.ascontiguousarray(t))
                 for t in (decay * scale, q_w_rows, k_w_t, cdec_rows))


def kernel(x, positions, pre_norm, w_in, conv_w, w_out, post_norm):
    depth = w_in.shape[0]
    b, s, d = x.shape
    assert d == D_MODEL and w_in.shape[1:] == (D_MODEL, 4 * D_CONV + 4 * D_RET)
    assert conv_w.shape[1:] == (CONV_K, D_CONV) and w_out.shape[1:] == (D_MIX, D_MODEL)
    assert s % SEQ_TILE == 0 and (b * s) % ROPE_TILE == 0
    half = HEAD_DIM // 2
    inv_freq = 1.0 / (ROPE_BASE ** (jnp.arange(half, dtype=F32) / half))
    inv_freq2 = jnp.concatenate([inv_freq, inv_freq]).reshape(1, HEAD_DIM)
    cosf, sinf, w_bf, wout_bf = _rope_tables(positions, inv_freq2, w_in, w_out)
    decay, qw, kw, cdec = _retention_tables()
    x = x.reshape(b * s, d)
    for layer in range(depth):
        x, w_bf, wout_bf = _layer_call(
            layer, x, cosf, sinf, pre_norm, w_bf, conv_w, wout_bf, post_norm, decay, qw, kw, cdec,
            w_in, w_out, s)
    return x.reshape(b, s, d)
```
